```python
import jax, jax.numpy as jnp
from jax import lax
import numpy as np

D_MODEL = 2048
BATCH = 4
SEQ = 2048
DEPTH = 2
DEC_BATCH = 8
DEC_SEQ = 32
PAST_LEN = 2048

CHUNK = 64
MIX_WIDTH = D_MODEL
GLA_WIDTH = MIX_WIDTH // 2
FOX_WIDTH = MIX_WIDTH - GLA_WIDTH
GLA_HEADS = 4
GLA_DV = GLA_WIDTH // GLA_HEADS
GLA_DK = GLA_DV // 2
GLA_RANK = 16
GLA_TAU = 16.0
FOX_HD = 128
FOX_HEADS = FOX_WIDTH // FOX_HD
FOX_QBLK = 128
D_FF = 4 * D_MODEL
EPS = 1e-6
PROJ_SIZES = (GLA_HEADS * GLA_DK, GLA_HEADS * GLA_DK, GLA_WIDTH, GLA_RANK, GLA_WIDTH,
              FOX_WIDTH, FOX_WIDTH, FOX_WIDTH, FOX_HEADS)
PROJ_WIDTH = 2 * GLA_HEADS * GLA_DK + 2 * GLA_WIDTH + GLA_RANK + 3 * FOX_WIDTH + FOX_HEADS

kernel_name = "hymba_gla_fox_stream_step"


def rmsnorm(x, g):
    xf = x.astype(jnp.float32)
    y = xf * lax.rsqrt(jnp.mean(xf * xf, axis=-1, keepdims=True) + EPS)
    return (y * g.astype(jnp.float32)).astype(x.dtype)


def project(h, w_in, w_gate_up, b_gate, b_fox_f):
    B, L, _ = h.shape
    z = jnp.einsum('bld,de->ble', h, w_in)
    idx = np.cumsum(PROJ_SIZES)[:-1].tolist()
    qg, kg, vg, glr, rg, qf, kf, vf, fl = jnp.split(z, idx, axis=-1)
    qg = qg.reshape(B, L, GLA_HEADS, GLA_DK) * (GLA_DK ** -0.5)
    kg = kg.reshape(B, L, GLA_HEADS, GLA_DK)
    vg = vg.reshape(B, L, GLA_HEADS, GLA_DV)
    g = jax.nn.log_sigmoid((jnp.einsum('blr,re->ble', glr, w_gate_up) + b_gate).astype(jnp.float32)) / GLA_TAU
    g = g.reshape(B, L, GLA_HEADS, GLA_DK)
    qf = qf.reshape(B, L, FOX_HEADS, FOX_HD)
    kf = kf.reshape(B, L, FOX_HEADS, FOX_HD)
    vf = vf.reshape(B, L, FOX_HEADS, FOX_HD)
    logf = jax.nn.log_sigmoid((fl + b_fox_f).astype(jnp.float32))
    return qg, kg, vg, g, rg, qf, kf, vf, logf


def gla_chunked(q, k, v, g, s0):
    B, L, H, DK = q.shape
    DV = v.shape[-1]
    pad = (-L) % CHUNK
    if pad:
        q, k, v, g = [jnp.pad(t, ((0, 0), (0, pad), (0, 0), (0, 0))) for t in (q, k, v, g)]
    N = (L + pad) // CHUNK

    def blocks(t):
        return t.reshape(B, N, CHUNK, H, t.shape[-1]).transpose(1, 0, 3, 2, 4)

    qb, kb, vb, gb = blocks(q), blocks(k), blocks(v), blocks(g)
    dt = q.dtype
    b = jnp.cumsum(gb, axis=3)
    b_last = b[:, :, :, -1:, :]
    q_i = (qb * jnp.exp(b)).astype(dt)
    k_i = (kb * jnp.exp(-b)).astype(dt)
    k_e = (kb * jnp.exp(b_last - b)).astype(dt)
    causal = jnp.tril(jnp.ones((CHUNK, CHUNK), dtype=bool))
    att = jnp.where(causal, jnp.einsum('nbhcd,nbhsd->nbhcs', q_i, k_i), 0)
    o_intra = jnp.einsum('nbhcs,nbhsv->nbhcv', att.astype(dt), vb)
    kv = jnp.einsum('nbhsd,nbhsv->nbhdv', k_e, vb).astype(s0.dtype)
    decay = jnp.exp(b_last[:, :, :, 0, :]).astype(s0.dtype)

    def step(S, inp):
        dec, kvn = inp
        return dec[..., None] * S + kvn, S

    s_final, s_starts = lax.scan(step, s0, (decay, kv))
    o_inter = jnp.einsum('nbhcd,nbhdv->nbhcv', q_i, s_starts.astype(dt))
    o = (o_intra + o_inter).transpose(1, 0, 3, 2, 4).reshape(B, N * CHUNK, H, DV)[:, :L]
    return o, s_final


def fox_attend(q, k, v, c_q, c_k, q_pos, k_pos):
    s = jnp.einsum('bqhd,bkhd->bhqk', q, k).astype(jnp.float32) * (FOX_HD ** -0.5)
    s = s + c_q.transpose(0, 2, 1)[:, :, :, None] - c_k.transpose(0, 2, 1)[:, :, None, :]
    s = jnp.where(k_pos[None, :] <= q_pos[:, None], s, -jnp.inf)
    p = jax.nn.softmax(s, axis=-1).astype(v.dtype)
    return jnp.einsum('bhqk,bkhd->bqhd', p, v)


def fox_prompt(q, k, v, logf):
    L = q.shape[1]
    c = jnp.cumsum(logf, axis=1)
    pos = jnp.arange(L)
    outs = []
    for i in range(L // FOX_QBLK):
        lo, hi = i * FOX_QBLK, (i + 1) * FOX_QBLK
        outs.append(fox_attend(q[:, lo:hi], k[:, :hi], v[:, :hi], c[:, lo:hi], c[:, :hi], pos[lo:hi], pos[:hi]))
    return jnp.concatenate(outs, axis=1)


def merge_heads(o_gla, r_g, o_fox, g_onorm, w_out):
    B, L = o_gla.shape[:2]
    o_gla = rmsnorm(o_gla, g_onorm.reshape(GLA_HEADS, GLA_DV)).reshape(B, L, GLA_WIDTH) * jax.nn.silu(r_g)
    o = jnp.concatenate([o_gla, o_fox.reshape(B, L, FOX_WIDTH)], axis=-1)
    return jnp.einsum('ble,ed->bld', o, w_out)


def sq_relu_mlp(h, w_up, w_down):
    u = jax.nn.relu(jnp.einsum('bld,df->blf', h, w_up))
    return jnp.einsum('blf,fd->bld', u * u, w_down)


def setup_inputs(seed: int = 0) -> dict:
    key = jax.random.key(seed)
    ks = jax.random.split(key, 20)

    def nrm(k, shape, scale=1.0):
        return jax.random.normal(k, shape, jnp.float32) * scale

    def gain(k, shape):
        return 1.0 + 0.05 * jax.random.normal(k, shape, jnp.float32)

    return {
        "x_prompt": nrm(ks[0], (BATCH, SEQ, D_MODEL)),
        "x_sample": nrm(ks[1], (DEC_BATCH, DEC_SEQ, D_MODEL)),
        "cache_fox_k": nrm(ks[2], (DEPTH, DEC_BATCH, PAST_LEN, FOX_HEADS, FOX_HD)),
        "cache_fox_v": nrm(ks[3], (DEPTH, DEC_BATCH, PAST_LEN, FOX_HEADS, FOX_HD)),
        "cache_fox_logf": jax.nn.log_sigmoid(nrm(ks[4], (DEPTH, DEC_BATCH, PAST_LEN, FOX_HEADS)) + 2.0),
        "state_gla": nrm(ks[5], (DEPTH, DEC_BATCH, GLA_HEADS, GLA_DK, GLA_DV), 0.5),
        "g_mix_pre": gain(ks[6], (DEPTH, D_MODEL)),
        "w_in": nrm(ks[7], (DEPTH, D_MODEL, PROJ_WIDTH), D_MODEL ** -0.5),
        "w_gla_gate_up": nrm(ks[8], (DEPTH, GLA_RANK, GLA_HEADS * GLA_DK), GLA_RANK ** -0.5),
        "b_gla_gate": nrm(ks[9], (DEPTH, GLA_HEADS * GLA_DK), 0.1),
        "b_fox_f": nrm(ks[10], (DEPTH, FOX_HEADS), 0.1),
        "g_gla_onorm": gain(ks[11], (DEPTH, GLA_WIDTH)),
        "w_out": nrm(ks[12], (DEPTH, MIX_WIDTH, D_MODEL), MIX_WIDTH ** -0.5),
        "g_mix_post": gain(ks[13], (DEPTH, D_MODEL)),
        "g_mlp_pre": gain(ks[14], (DEPTH, D_MODEL)),
        "w_mlp_up": nrm(ks[15], (DEPTH, D_MODEL, D_FF), D_MODEL ** -0.5),
        "w_mlp_down": nrm(ks[16], (DEPTH, D_FF, D_MODEL), D_FF ** -0.5),
        "g_mlp_post": gain(ks[17], (DEPTH, D_MODEL)),
    }


def reference(x_prompt, x_sample, cache_fox_k, cache_fox_v, cache_fox_logf, state_gla,
              g_mix_pre, w_in, w_gla_gate_up, b_gla_gate, b_fox_f, g_gla_onorm, w_out,
              g_mix_post, g_mlp_pre, w_mlp_up, w_mlp_down, g_mlp_post):
    past_len = cache_fox_k.shape[2]
    yp, ys = x_prompt, x_sample
    kp, vp, fp, sp = [], [], [], []
    ksm, vsm, fsm, ssm = [], [], [], []
    for l in range(DEPTH):
        h = rmsnorm(yp, g_mix_pre[l])
        qg, kg, vg, g, rg, qf, kf, vf, logf = project(h, w_in[l], w_gla_gate_up[l], b_gla_gate[l], b_fox_f[l])
        s0 = jnp.zeros((yp.shape[0], GLA_HEADS, GLA_DK, GLA_DV), yp.dtype)
        o_gla, s_new = gla_chunked(qg, kg, vg, g, s0)
        o_fox = fox_prompt(qf, kf, vf, logf)
        yp = yp + rmsnorm(merge_heads(o_gla, rg, o_fox, g_gla_onorm[l], w_out[l]), g_mix_post[l])
        yp = yp + rmsnorm(sq_relu_mlp(rmsnorm(yp, g_mlp_pre[l]), w_mlp_up[l], w_mlp_down[l]), g_mlp_post[l])
        kp.append(kf); vp.append(vf); fp.append(logf); sp.append(s_new)

        h = rmsnorm(ys, g_mix_pre[l])
        qg, kg, vg, g, rg, qf, kf, vf, logf = project(h, w_in[l], w_gla_gate_up[l], b_gla_gate[l], b_fox_f[l])
        o_gla, s_new = gla_chunked(qg, kg, vg, g, state_gla[l])
        T = qf.shape[1]
        k_all = jnp.concatenate([cache_fox_k[l], kf], axis=1)
        v_all = jnp.concatenate([cache_fox_v[l], vf], axis=1)
        c_all = jnp.cumsum(jnp.concatenate([cache_fox_logf[l].astype(jnp.float32), logf], axis=1), axis=1)
        o_fox = fox_attend(qf, k_all, v_all, c_all[:, past_len:], c_all,
                           past_len + jnp.arange(T), jnp.arange(past_len + T))
        ys = ys + rmsnorm(merge_heads(o_gla, rg, o_fox, g_gla_onorm[l], w_out[l]), g_mix_post[l])
        ys = ys + rmsnorm(sq_relu_mlp(rmsnorm(ys, g_mlp_pre[l]), w_mlp_up[l], w_mlp_down[l]), g_mlp_post[l])
        ksm.append(kf); vsm.append(vf); fsm.append(logf); ssm.append(s_new)

    new_fox_k_prompt = jnp.stack(kp)
    new_fox_v_prompt = jnp.stack(vp)
    new_fox_logf_prompt = jnp.stack(fp)
    new_state_gla_prompt = jnp.stack(sp)
    new_fox_k_sample = jnp.stack(ksm)
    new_fox_v_sample = jnp.stack(vsm)
    new_fox_logf_sample = jnp.stack(fsm)
    new_state_gla_sample = jnp.stack(ssm)
    return (yp, ys, new_fox_k_prompt, new_fox_v_prompt, new_fox_logf_prompt, new_state_gla_prompt,
            new_fox_k_sample, new_fox_v_sample, new_fox_logf_sample, new_state_gla_sample)
```

```python
import functools

import jax
import jax.numpy as jnp
from jax import lax
from jax.experimental import pallas as pl
from jax.experimental.pallas import tpu as pltpu

F32 = jnp.float32
BF16 = jnp.bfloat16
HIGHEST = lax.Precision.HIGHEST

D_MODEL = 2048
DEPTH = 2
CHUNK = 64
GLA_HEADS = 4
GLA_DK = 128
GLA_DV = 256
GLA_QK = GLA_HEADS * GLA_DK
GLA_WIDTH = GLA_HEADS * GLA_DV
GLA_RANK = 16
GLA_TAU = 16.0
FOX_HEADS = 8
FOX_HD = 128
FOX_WIDTH = FOX_HEADS * FOX_HD
D_FF = 4 * D_MODEL
EPS = 1e-6
NEG = -1e30

O_GLR = 2 * GLA_QK + GLA_WIDTH
O_RG = O_GLR + GLA_RANK
O_FL = O_RG + GLA_WIDTH + 3 * FOX_WIDTH
PROJ_WIDTH = O_FL + FOX_HEADS
MAIN_WIDTH = PROJ_WIDTH - GLA_RANK - FOX_HEADS
LANES = 128
SMALL_FL = GLA_RANK

VMEM_LIMIT = 52 * 1024 * 1024
NT_DIMS = (((1,), (1,)), ((), ()))
TN_DIMS = (((0,), (0,)), ((), ()))


def _log_sigmoid(x):
    return jnp.minimum(x, 0.0) - jnp.log1p(jnp.exp(-jnp.abs(x)))


def _rms(x):
    return x * lax.rsqrt(jnp.mean(x * x, axis=-1, keepdims=True) + EPS)


IN_TN = 512
_MAIN_GROUPS = ((0, 1), (1, 1), (2, 2), (4, 2), (6, 2), (8, 2), (10, 2))


def _in_proj_kernel(x_ref, g_ref, wm_ref, ws_ref, wflt_ref, wgu_ref, bg_ref, bf_ref, bft_ref,
                    qg_ref, kg_ref, vg_ref, rg_ref, qf_ref, kf_ref, vf_ref,
                    gate_ref, logf_ref, logft_ref, h_scr):
    j = pl.program_id(1)

    @pl.when(j == 0)
    def _():
        hb = (_rms(x_ref[...]) * g_ref[...]).astype(BF16)
        h_scr[...] = hb
        zs = jnp.dot(hb, ws_ref[...], preferred_element_type=F32)
        gpre = jnp.dot(zs.astype(BF16), wgu_ref[...], preferred_element_type=F32) + bg_ref[...]
        gate_ref[...] = _log_sigmoid(gpre) / GLA_TAU
        logf_ref[...] = _log_sigmoid(zs[:, SMALL_FL:SMALL_FL + FOX_HEADS] + bf_ref[...])
        zst = lax.dot_general(wflt_ref[...], hb, NT_DIMS, preferred_element_type=F32)
        logft_ref[...] = _log_sigmoid(zst[:FOX_HEADS, :] + bft_ref[...])

    acc = jnp.dot(h_scr[...], wm_ref[...], preferred_element_type=F32)
    fox_scale = FOX_HD ** -0.5
    outs = (qg_ref, kg_ref, vg_ref, rg_ref, qf_ref, kf_ref, vf_ref)
    for ref, (start, n) in zip(outs, _MAIN_GROUPS):
        @pl.when((j >= start) & (j < start + n))
        def _(ref=ref):
            val = acc * fox_scale if ref is qf_ref else acc
            ref[...] = val.astype(ref.dtype)


def _in_proj(x, g_pre, w_main, w_small, w_flt, w_gu, b_gate, b_f, b_ft, tm):
    m = x.shape[0]
    nj = MAIN_WIDTH // IN_TN
    const = lambda i, j: (0, 0)
    widths = (GLA_QK, GLA_QK, GLA_WIDTH, GLA_WIDTH, FOX_WIDTH, FOX_WIDTH, FOX_WIDTH)
    dtypes = (F32, F32, BF16, F32, BF16, F32, F32)
    out_shape = [jax.ShapeDtypeStruct((m, w), dt) for w, dt in zip(widths, dtypes)]
    out_specs = [
        pl.BlockSpec((tm, IN_TN), lambda i, j, s=s, n=n: (i, jnp.clip(j - s, 0, n - 1)))
        for (s, n) in _MAIN_GROUPS
    ]
    out_shape += [jax.ShapeDtypeStruct((m, GLA_QK), F32),
                  jax.ShapeDtypeStruct((m, FOX_HEADS), F32),
                  jax.ShapeDtypeStruct((FOX_HEADS, m), F32)]
    out_specs += [pl.BlockSpec((tm, GLA_QK), lambda i, j: (i, 0)),
                  pl.BlockSpec((tm, FOX_HEADS), lambda i, j: (i, 0)),
                  pl.BlockSpec((FOX_HEADS, tm), lambda i, j: (0, i))]
    return pl.pallas_call(
        _in_proj_kernel,
        grid=(m // tm, nj),
        in_specs=[
            pl.BlockSpec((tm, D_MODEL), lambda i, j: (i, 0)),
            pl.BlockSpec((1, D_MODEL), const),
            pl.BlockSpec((D_MODEL, IN_TN), lambda i, j: (0, j)),
            pl.BlockSpec((D_MODEL, LANES), const),
            pl.BlockSpec((16, D_MODEL), const),
            pl.BlockSpec((LANES, GLA_QK), const),
            pl.BlockSpec((1, GLA_QK), const),
            pl.BlockSpec((1, FOX_HEADS), const),
            pl.BlockSpec((FOX_HEADS, 1), const),
        ],
        out_specs=out_specs,
        out_shape=out_shape,
        scratch_shapes=[pltpu.VMEM((tm, D_MODEL), BF16)],
        compiler_params=pltpu.CompilerParams(
            dimension_semantics=("arbitrary", "arbitrary"), vmem_limit_bytes=VMEM_LIMIT),
        name="in_proj",
    )(x, g_pre, w_main, w_small, w_flt, w_gu, b_gate, b_f, b_ft)


def _gla_kernel(*refs, chunk, nchunks, has_s0):
    if has_s0:
        q_ref, k_ref, v_ref, g_ref, r_ref, gon_ref, s0_ref, o_ref, sout_ref, st = refs
    else:
        q_ref, k_ref, v_ref, g_ref, r_ref, gon_ref, o_ref, sout_ref, st = refs
    t = pl.program_id(2)

    @pl.when(t == 0)
    def _():
        if has_s0:
            st[...] = s0_ref[...].T
        else:
            st[...] = jnp.zeros_like(st)

    row = lax.broadcasted_iota(jnp.int32, (chunk, chunk), 0)
    col = lax.broadcasted_iota(jnp.int32, (chunk, chunk), 1)
    causal = col <= row
    tri = causal.astype(F32)
    qscale = GLA_DK ** -0.5
    gon = gon_ref[...]
    for c in range(nchunks):
        sl = pl.ds(c * chunk, chunk)
        b = jnp.dot(tri, g_ref[sl, :], precision=HIGHEST, preferred_element_type=F32)
        b_last = b[chunk - 1:chunk, :]
        k = k_ref[sl, :]
        q_i = (q_ref[sl, :] * qscale * jnp.exp(b)).astype(BF16)
        k_i = (k * jnp.exp(-b)).astype(BF16)
        k_e = (k * jnp.exp(b_last - b)).astype(BF16)
        v = v_ref[sl, :]
        att = lax.dot_general(q_i, k_i, NT_DIMS, preferred_element_type=F32)
        att = jnp.where(causal, att, 0.0).astype(BF16)
        s_old = st[...]
        o = jnp.dot(att, v, preferred_element_type=F32)
        o = o + lax.dot_general(q_i, s_old.astype(BF16), NT_DIMS, preferred_element_type=F32)
        kv_t = lax.dot_general(v, k_e, TN_DIMS, preferred_element_type=F32)
        st[...] = s_old * jnp.exp(b_last) + kv_t
        r = r_ref[sl, :]
        o_ref[sl, :] = (_rms(o) * gon * (r * jax.nn.sigmoid(r))).astype(o_ref.dtype)

    @pl.when(t == pl.num_programs(2) - 1)
    def _():
        sout_ref[...] = st[...].T


def _gla(qg, kg, vg, gate, rg, g_onorm, s0, batch, seq, rows, chunk):
    has_s0 = s0 is not None
    nchunks = rows // chunk
    r3 = lambda a: a.reshape(batch, seq, a.shape[-1])
    tok = lambda w: pl.BlockSpec((None, rows, w), lambda b, h, t: (b, t, h))
    state_spec = pl.BlockSpec((None, None, GLA_DK, GLA_DV), lambda b, h, t: (b, h, 0, 0))
    in_specs = [tok(GLA_DK), tok(GLA_DK), tok(GLA_DV), tok(GLA_DK), tok(GLA_DV),
                pl.BlockSpec((1, GLA_DV), lambda b, h, t: (0, h))]
    args = [r3(qg), r3(kg), r3(vg), r3(gate), r3(rg), g_onorm]
    if has_s0:
        in_specs.append(state_spec)
        args.append(s0)
    return pl.pallas_call(
        functools.partial(_gla_kernel, chunk=chunk, nchunks=nchunks, has_s0=has_s0),
        grid=(batch, GLA_HEADS, seq // rows),
        in_specs=in_specs,
        out_specs=[tok(GLA_DV), state_spec],
        out_shape=[jax.ShapeDtypeStruct((batch, seq, GLA_WIDTH), BF16),
                   jax.ShapeDtypeStruct((batch, GLA_HEADS, GLA_DK, GLA_DV), F32)],
        scratch_shapes=[pltpu.VMEM((GLA_DV, GLA_DK), F32)],
        compiler_params=pltpu.CompilerParams(
            dimension_semantics=("arbitrary", "arbitrary", "arbitrary"), vmem_limit_bytes=VMEM_LIMIT),
        name="gla",
    )(*args)


def _cumsum_kernel(x_ref, o_ref, *, nblk):
    x = x_ref[...]
    length = nblk * LANES
    r = lax.broadcasted_iota(jnp.int32, (LANES, LANES), 0)
    c = lax.broadcasted_iota(jnp.int32, (LANES, LANES), 1)
    triu = (r <= c).astype(F32)
    li = lax.broadcasted_iota(jnp.int32, (length, LANES), 0)
    ji = lax.broadcasted_iota(jnp.int32, (length, LANES), 1)
    before = (li < ji * LANES).astype(F32)
    offs = jnp.dot(x, before, precision=HIGHEST, preferred_element_type=F32)
    for j in range(nblk):
        blk = x[:, j * LANES:(j + 1) * LANES]
        o_ref[:, j * LANES:(j + 1) * LANES] = (
            jnp.dot(blk, triu, precision=HIGHEST, preferred_element_type=F32) + offs[:, j:j + 1])


def _cumsum(xt, nseg, seglen):
    return pl.pallas_call(
        functools.partial(_cumsum_kernel, nblk=seglen // LANES),
        grid=(nseg,),
        in_specs=[pl.BlockSpec((FOX_HEADS, seglen), lambda s: (0, s))],
        out_specs=pl.BlockSpec((FOX_HEADS, seglen), lambda s: (0, s)),
        out_shape=jax.ShapeDtypeStruct(xt.shape, F32),
        compiler_params=pltpu.CompilerParams(dimension_semantics=("arbitrary",)),
        name="cumsum",
    )(xt)


def _pick_row(block, h):
    sel = lax.broadcasted_iota(jnp.int32, (FOX_HEADS, 1), 0) == h
    return jnp.sum(jnp.where(sel, block, 0.0), axis=0, keepdims=True)


def _pick_col(block, h):
    sel = lax.broadcasted_iota(jnp.int32, (1, FOX_HEADS), 1) == h
    return jnp.sum(jnp.where(sel, block, 0.0), axis=1, keepdims=True)


def _fox_prompt_kernel(q_ref, k_ref, v_ref, crow_ref, ccol_ref, o_ref, kb, vb, ck, *, tq, nq):
    h = pl.program_id(1)
    qi = pl.program_id(2)

    @pl.when(qi == 0)
    def _():
        crow = _pick_row(crow_ref[...], h)
        for jb in range(nq):
            kb[jb] = k_ref[jb * tq:(jb + 1) * tq, :].astype(BF16)
            vb[jb] = v_ref[jb * tq:(jb + 1) * tq, :].astype(BF16)
            ck[jb] = crow[:, jb * tq:(jb + 1) * tq]

    cq = _pick_col(ccol_ref[...], h)
    q = q_ref[...]
    row = lax.broadcasted_iota(jnp.int32, (tq, tq), 0)
    col = lax.broadcasted_iota(jnp.int32, (tq, tq), 1)

    def block(j, carry, masked):
        m, l, acc = carry
        s = lax.dot_general(q, kb[j], NT_DIMS, preferred_element_type=F32)
        s = s + (cq - ck[j])
        if masked:
            s = jnp.where(col <= row, s, NEG)
        m_new = jnp.maximum(m, jnp.max(s, axis=1, keepdims=True))
        alpha = jnp.exp(m - m_new)
        p = jnp.exp(s - m_new)
        l = alpha * l + jnp.sum(p, axis=1, keepdims=True)
        acc = alpha * acc + jnp.dot(p.astype(BF16), vb[j], preferred_element_type=F32)
        return m_new, l, acc

    init = (jnp.full((tq, 1), NEG, F32), jnp.zeros((tq, 1), F32), jnp.zeros((tq, FOX_HD), F32))
    carry = lax.fori_loop(0, qi, lambda j, cr: block(j, cr, False), init)
    _, l, acc = block(qi, carry, True)
    o_ref[...] = (acc / l).astype(o_ref.dtype)


def _fox_prompt(qf, kf, vf, c_row, c_col, batch, seq, tq):
    nq = seq // tq
    r3 = lambda a: a.reshape(batch, seq, a.shape[-1])
    return pl.pallas_call(
        functools.partial(_fox_prompt_kernel, tq=tq, nq=nq),
        grid=(batch, FOX_HEADS, nq),
        in_specs=[
            pl.BlockSpec((None, tq, FOX_HD), lambda b, h, i: (b, i, h)),
            pl.BlockSpec((None, seq, FOX_HD), lambda b, h, i: (b, 0, h)),
            pl.BlockSpec((None, seq, FOX_HD), lambda b, h, i: (b, 0, h)),
            pl.BlockSpec((FOX_HEADS, seq), lambda b, h, i: (0, b)),
            pl.BlockSpec((None, tq, FOX_HEADS), lambda b, h, i: (b, i, 0)),
        ],
        out_specs=pl.BlockSpec((None, tq, FOX_HD), lambda b, h, i: (b, i, h)),
        out_shape=jax.ShapeDtypeStruct((batch, seq, FOX_WIDTH), BF16),
        scratch_shapes=[pltpu.VMEM((nq, tq, FOX_HD), BF16),
                        pltpu.VMEM((nq, tq, FOX_HD), BF16),
                        pltpu.VMEM((nq, 1, tq), F32)],
        compiler_params=pltpu.CompilerParams(
            dimension_semantics=("arbitrary", "arbitrary", "arbitrary"), vmem_limit_bytes=VMEM_LIMIT),
        name="fox_prompt",
    )(r3(qf), r3(kf), r3(vf), c_row, c_col)


def _fox_sample_kernel(q_ref, kc_ref, vc_ref, kn_ref, vn_ref, crow_ref, cq_ref, o_ref, *, past, new):
    h = pl.program_id(1)
    crow = _pick_row(crow_ref[...], h)
    cq = _pick_col(cq_ref[...], h)
    q = q_ref[...]
    s_c = lax.dot_general(q, kc_ref[...].astype(BF16), NT_DIMS, preferred_element_type=F32)
    s_c = s_c + (cq - crow[:, :past])
    s_n = lax.dot_general(q, kn_ref[...].astype(BF16), NT_DIMS, preferred_element_type=F32)
    s_n = s_n + (cq - crow[:, past:past + new])
    row = lax.broadcasted_iota(jnp.int32, (new, new), 0)
    col = lax.broadcasted_iota(jnp.int32, (new, new), 1)
    s_n = jnp.where(col <= row, s_n, NEG)
    m = jnp.maximum(jnp.max(s_c, axis=1, keepdims=True), jnp.max(s_n, axis=1, keepdims=True))
    p_c = jnp.exp(s_c - m)
    p_n = jnp.exp(s_n - m)
    l = jnp.sum(p_c, axis=1, keepdims=True) + jnp.sum(p_n, axis=1, keepdims=True)
    o = jnp.dot(p_c.astype(BF16), vc_ref[...].astype(BF16), preferred_element_type=F32)
    o = o + jnp.dot(p_n.astype(BF16), vn_ref[...].astype(BF16), preferred_element_type=F32)
    o_ref[...] = (o / l).astype(o_ref.dtype)


def _fox_sample(qf, k_cache, v_cache, layer, kf, vf, c_row, c_q, batch, new):
    past = k_cache.shape[2]
    seg = c_row.shape[1] // batch
    r3 = lambda a, n: a.reshape(batch, n, FOX_WIDTH)
    head = lambda n: pl.BlockSpec((None, n, FOX_HD), lambda b, h: (b, 0, h))
    hist = pl.BlockSpec((None, None, past, FOX_HD), lambda b, h: (layer, b, 0, h))
    return pl.pallas_call(
        functools.partial(_fox_sample_kernel, past=past, new=new),
        grid=(batch, FOX_HEADS),
        in_specs=[head(new), hist, hist, head(new), head(new),
                  pl.BlockSpec((FOX_HEADS, seg), lambda b, h: (0, b)),
                  pl.BlockSpec((None, new, FOX_HEADS), lambda b, h: (b, 0, 0))],
        out_specs=head(new),
        out_shape=jax.ShapeDtypeStruct((batch, new, FOX_WIDTH), BF16),
        compiler_params=pltpu.CompilerParams(
            dimension_semantics=("arbitrary", "arbitrary"), vmem_limit_bytes=VMEM_LIMIT),
        name="fox_sample",
    )(r3(qf, new), k_cache, v_cache, r3(kf, new), r3(vf, new), c_row, c_q)


def _out_proj_kernel(og_ref, of_ref, w_ref, x_ref, gpost_ref, gpre_ref, x1_ref, h2_ref):
    y = jnp.dot(og_ref[...], w_ref[:GLA_WIDTH, :], preferred_element_type=F32)
    y = y + jnp.dot(of_ref[...], w_ref[GLA_WIDTH:, :], preferred_element_type=F32)
    x1 = x_ref[...] + _rms(y) * gpost_ref[...]
    x1_ref[...] = x1
    h2_ref[...] = (_rms(x1) * gpre_ref[...]).astype(BF16)


def _out_proj(o_gla, o_fox, w_out, x, g_post, g_mlp_pre, tm):
    m = x.shape[0]
    const = lambda i: (0, 0)
    rowblk = lambda w: pl.BlockSpec((tm, w), lambda i: (i, 0))
    return pl.pallas_call(
        _out_proj_kernel,
        grid=(m // tm,),
        in_specs=[rowblk(GLA_WIDTH), rowblk(FOX_WIDTH),
                  pl.BlockSpec((D_MODEL, D_MODEL), const),
                  rowblk(D_MODEL),
                  pl.BlockSpec((1, D_MODEL), const), pl.BlockSpec((1, D_MODEL), const)],
        out_specs=[rowblk(D_MODEL), rowblk(D_MODEL)],
        out_shape=[jax.ShapeDtypeStruct((m, D_MODEL), F32), jax.ShapeDtypeStruct((m, D_MODEL), BF16)],
        compiler_params=pltpu.CompilerParams(
            dimension_semantics=("arbitrary",), vmem_limit_bytes=VMEM_LIMIT),
        name="out_proj",
    )(o_gla.reshape(m, GLA_WIDTH), o_fox.reshape(m, FOX_WIDTH), w_out, x, g_post, g_mlp_pre)


MLP_TF = 512


def _mlp_kernel(h_ref, wu_ref, wd_ref, x1_ref, g_ref, o_ref, acc):
    j = pl.program_id(1)
    u = jnp.maximum(jnp.dot(h_ref[...], wu_ref[...], preferred_element_type=F32), 0.0)
    part = jnp.dot((u * u).astype(BF16), wd_ref[...], preferred_element_type=F32)

    @pl.when(j == 0)
    def _():
        acc[...] = part

    @pl.when(j > 0)
    def _():
        acc[...] += part

    @pl.when(j == pl.num_programs(1) - 1)
    def _():
        o_ref[...] = x1_ref[...] + _rms(acc[...]) * g_ref[...]


def _mlp(h2, w_up, w_down, x1, g_post, tm):
    m = x1.shape[0]
    return pl.pallas_call(
        _mlp_kernel,
        grid=(m // tm, D_FF // MLP_TF),
        in_specs=[pl.BlockSpec((tm, D_MODEL), lambda i, j: (i, 0)),
                  pl.BlockSpec((D_MODEL, MLP_TF), lambda i, j: (0, j)),
                  pl.BlockSpec((MLP_TF, D_MODEL), lambda i, j: (j, 0)),
                  pl.BlockSpec((tm, D_MODEL), lambda i, j: (i, 0)),
                  pl.BlockSpec((1, D_MODEL), lambda i, j: (0, 0))],
        out_specs=pl.BlockSpec((tm, D_MODEL), lambda i, j: (i, 0)),
        out_shape=jax.ShapeDtypeStruct((m, D_MODEL), F32),
        scratch_shapes=[pltpu.VMEM((tm, D_MODEL), F32)],
        compiler_params=pltpu.CompilerParams(
            dimension_semantics=("arbitrary", "arbitrary"), vmem_limit_bytes=VMEM_LIMIT),
        name="mlp",
    )(h2, w_up, w_down, x1, g_post)


def _layer_weights(l, g_mix_pre, w_in, w_gla_gate_up, b_gla_gate, b_fox_f, g_gla_onorm, w_out,
                   g_mix_post, g_mlp_pre, w_mlp_up, w_mlp_down, g_mlp_post):
    w = w_in[l]
    w_fl = w[:, O_FL:]
    row = lambda v: v.reshape(1, -1)
    return dict(
        g_pre=row(g_mix_pre[l]),
        w_main=jnp.concatenate([w[:, :O_GLR], w[:, O_RG:O_FL]], axis=1).astype(BF16),
        w_small=jnp.pad(jnp.concatenate([w[:, O_GLR:O_RG], w_fl], axis=1),
                        ((0, 0), (0, LANES - GLA_RANK - FOX_HEADS))).astype(BF16),
        w_flt=jnp.pad(w_fl.T, ((0, 16 - FOX_HEADS), (0, 0))).astype(BF16),
        w_gu=jnp.pad(w_gla_gate_up[l], ((0, LANES - GLA_RANK), (0, 0))).astype(BF16),
        b_gate=row(b_gla_gate[l]),
        b_f=row(b_fox_f[l]),
        b_ft=b_fox_f[l].reshape(-1, 1),
        g_onorm=row(g_gla_onorm[l]),
        w_out=w_out[l].astype(BF16),
        g_post=row(g_mix_post[l]),
        g_mlp_pre=row(g_mlp_pre[l]),
        w_up=w_mlp_up[l].astype(BF16),
        w_down=w_mlp_down[l].astype(BF16),
        g_mlp_post=row(g_mlp_post[l]),
    )


def _layer(x, wts, batch, seq, tm, gla_rows, gla_chunk, s0=None, cache=None):
    qg, kg, vg, rg, qf, kf, vf, gate, logf, logft = _in_proj(
        x, wts["g_pre"], wts["w_main"], wts["w_small"], wts["w_flt"], wts["w_gu"],
        wts["b_gate"], wts["b_f"], wts["b_ft"], tm)
    o_gla, s_new = _gla(qg, kg, vg, gate, rg, wts["g_onorm"], s0, batch, seq, gla_rows, gla_chunk)
    if cache is None:
        c_row = _cumsum(logft, batch, seq)
        c_col = c_row.T.reshape(batch, seq, FOX_HEADS)
        o_fox = _fox_prompt(qf, kf, vf, c_row, c_col, batch, seq, tq=256)
    else:
        k_cache, v_cache, logf_cache, layer = cache
        past = k_cache.shape[2]
        total = past + seq
        seg = -(-total // LANES) * LANES
        lf = jnp.concatenate([logf_cache, logf.reshape(batch, seq, FOX_HEADS)], axis=1)
        lf = jnp.pad(lf.transpose(2, 0, 1), ((0, 0), (0, 0), (0, seg - total)))
        c_row = _cumsum(lf.reshape(FOX_HEADS, batch * seg), batch, seg)
        c_q = c_row.reshape(FOX_HEADS, batch, seg)[:, :, past:total].transpose(1, 2, 0)
        o_fox = _fox_sample(qf, k_cache, v_cache, layer, kf, vf, c_row, c_q, batch, seq)
    x1, h2 = _out_proj(o_gla, o_fox, wts["w_out"], x, wts["g_post"], wts["g_mlp_pre"], min(tm, 256))
    x2 = _mlp(h2, wts["w_up"], wts["w_down"], x1, wts["g_mlp_post"], tm)
    shp = (batch, seq, FOX_HEADS, FOX_HD)
    return x2, kf.reshape(shp), vf.reshape(shp), logf.reshape(batch, seq, FOX_HEADS), s_new


def kernel(x_prompt, x_sample, cache_fox_k, cache_fox_v, cache_fox_logf, state_gla, g_mix_pre, w_in,
           w_gla_gate_up, b_gla_gate, b_fox_f, g_gla_onorm, w_out, g_mix_post, g_mlp_pre, w_mlp_up,
           w_mlp_down, g_mlp_post):
    pb, pl_, _ = x_prompt.shape
    sb, sl, _ = x_sample.shape
    yp = x_prompt.reshape(pb * pl_, D_MODEL)
    ys = x_sample.reshape(sb * sl, D_MODEL)
    prompt_outs, sample_outs = [], []
    past = cache_fox_k.shape[2]
    k_hist = cache_fox_k.reshape(DEPTH, sb, past, FOX_WIDTH)
    v_hist = cache_fox_v.reshape(DEPTH, sb, past, FOX_WIDTH)
    for l in range(DEPTH):
        wts = _layer_weights(l, g_mix_pre, w_in, w_gla_gate_up, b_gla_gate, b_fox_f, g_gla_onorm,
                             w_out, g_mix_post, g_mlp_pre, w_mlp_up, w_mlp_down, g_mlp_post)
        yp, *outs = _layer(yp, wts, pb, pl_, tm=512, gla_rows=512, gla_chunk=CHUNK)
        prompt_outs.append(outs)
        cache = (k_hist, v_hist, cache_fox_logf[l], l)
        ys, *outs = _layer(ys, wts, sb, sl, tm=sb * sl, gla_rows=sl, gla_chunk=sl,
                           s0=state_gla[l], cache=cache)
        sample_outs.append(outs)
    stack = lambda outs, i: jnp.stack([o[i] for o in outs])
    return (yp.reshape(x_prompt.shape), ys.reshape(x_sample.shape),
            stack(prompt_outs, 0), stack(prompt_outs, 1), stack(prompt_outs, 2), stack(prompt_outs, 3),
            stack(sample_outs, 0), stack(sample_outs, 1), stack(sample_outs, 2), stack(sample_outs, 3))
```

```python
import functools

import jax
import jax.numpy as jnp
from jax import lax
from jax.experimental import pallas as pl
from jax.experimental.pallas import tpu as pltpu

F32 = jnp.float32
BF16 = jnp.bfloat16
HIGHEST = lax.Precision.HIGHEST

D_MODEL = 2048
DEPTH = 2
CHUNK = 64
GLA_HEADS = 4
GLA_DK = 128
GLA_DV = 256
GLA_QK = GLA_HEADS * GLA_DK
GLA_WIDTH = GLA_HEADS * GLA_DV
GLA_RANK = 16
GLA_TAU = 16.0
FOX_HEADS = 8
FOX_HD = 128
FOX_WIDTH = FOX_HEADS * FOX_HD
D_FF = 4 * D_MODEL
EPS = 1e-6
NEG = -1e30

O_GLR = 2 * GLA_QK + GLA_WIDTH
O_RG = O_GLR + GLA_RANK
O_FL = O_RG + GLA_WIDTH + 3 * FOX_WIDTH
PROJ_WIDTH = O_FL + FOX_HEADS
MAIN_WIDTH = PROJ_WIDTH - GLA_RANK - FOX_HEADS
LANES = 128
SUBLANES = 8
SMALL_FL = GLA_RANK

VMEM_LIMIT = 52 * 1024 * 1024
NT_DIMS = (((1,), (1,)), ((), ()))
TN_DIMS = (((0,), (0,)), ((), ()))


def _log_sigmoid(x):
    return jnp.minimum(x, 0.0) - jnp.log1p(jnp.exp(-jnp.abs(x)))


def _rms(x):
    return x * lax.rsqrt(jnp.mean(x * x, axis=-1, keepdims=True) + EPS)


def _lane_fold(x, op):
    parts = [x[:, i * LANES:(i + 1) * LANES] for i in range(x.shape[1] // LANES)]
    while len(parts) > 1:
        parts = [op(parts[i], parts[i + 1]) for i in range(0, len(parts), 2)]
    return parts[0]


IN_TN = 512
HEADS_PER_TN = IN_TN // FOX_HD


def _in_proj_kernel(x_ref, g_ref, wm_ref, ws_ref, wflt_ref, wgu_ref, bg_ref, bf_ref, bft_ref,
                    qg_ref, kg_ref, vg_ref, rg_ref, gate_ref, logf_ref, logft_ref,
                    qh_ref, kh_ref, vh_ref, kf_ref, vf_ref, h_scr, *, nb, lt):
    j = pl.program_id(1)
    tm = nb * lt

    @pl.when(j == 0)
    def _():
        hb = (_rms(x_ref[...]) * g_ref[...]).astype(BF16)
        h_scr[...] = hb
        zs = jnp.dot(hb, ws_ref[...], preferred_element_type=F32)
        gpre = jnp.dot(zs.astype(BF16), wgu_ref[...], preferred_element_type=F32) + bg_ref[...]
        gate_ref[...] = _log_sigmoid(gpre) / GLA_TAU
        logf_ref[...] = _log_sigmoid(zs[:, SMALL_FL:SMALL_FL + FOX_HEADS] + bf_ref[...])
        zst = lax.dot_general(wflt_ref[...], hb, NT_DIMS, preferred_element_type=F32)
        logft_ref[...] = _log_sigmoid(zst[:FOX_HEADS, :] + bft_ref[...])

    def mm():
        return jnp.dot(h_scr[...], wm_ref[...], preferred_element_type=F32)

    def heads(ref, val):
        for hh in range(HEADS_PER_TN):
            ref[:, hh] = val[:, hh * FOX_HD:(hh + 1) * FOX_HD].astype(BF16).reshape(nb, lt, FOX_HD)

    def rows(ref, val, half):
        for hh in range(HEADS_PER_TN):
            ref[pl.ds(half * HEADS_PER_TN + hh, tm, stride=FOX_HEADS), :] = (
                val[:, hh * FOX_HD:(hh + 1) * FOX_HD])

    @pl.when(j == 0)
    def _():
        qg_ref[...] = mm()

    @pl.when(j == 1)
    def _():
        kg_ref[...] = mm()

    @pl.when((j == 2) | (j == 3))
    def _():
        vg_ref[...] = mm().astype(BF16)

    @pl.when((j == 4) | (j == 5))
    def _():
        rg_ref[...] = mm()

    @pl.when((j == 6) | (j == 7))
    def _():
        heads(qh_ref, mm() * (FOX_HD ** -0.5))

    for half in range(2):
        @pl.when(j == 8 + half)
        def _(half=half):
            val = mm()
            heads(kh_ref, val)
            rows(kf_ref, val, half)

        @pl.when(j == 10 + half)
        def _(half=half):
            val = mm()
            heads(vh_ref, val)
            rows(vf_ref, val, half)


def _in_proj(x, wts, batch, seq, tm):
    m = x.shape[0]
    nj = MAIN_WIDTH // IN_TN
    lt = min(tm, seq)
    nb = tm // lt
    nt = seq // lt
    const = lambda i, j: (0, 0)
    col = lambda s, n: (lambda i, j: (i, jnp.clip(j - s, 0, n - 1)))
    hd = lambda s: (lambda i, j: (i // nt, jnp.clip(j - s, 0, 1), i % nt, 0))
    head_shape = jax.ShapeDtypeStruct((batch, FOX_HEADS, seq, FOX_HD), BF16)
    head_spec = lambda s: pl.BlockSpec((nb, HEADS_PER_TN, lt, FOX_HD), hd(s))
    out_shape = [
        jax.ShapeDtypeStruct((m, GLA_QK), F32), jax.ShapeDtypeStruct((m, GLA_QK), F32),
        jax.ShapeDtypeStruct((m, GLA_WIDTH), BF16), jax.ShapeDtypeStruct((m, GLA_WIDTH), F32),
        jax.ShapeDtypeStruct((m, GLA_QK), F32), jax.ShapeDtypeStruct((m, FOX_HEADS), F32),
        jax.ShapeDtypeStruct((FOX_HEADS, m), F32),
        head_shape, head_shape, head_shape,
        jax.ShapeDtypeStruct((m * FOX_HEADS, FOX_HD), F32),
        jax.ShapeDtypeStruct((m * FOX_HEADS, FOX_HD), F32),
    ]
    out_specs = [
        pl.BlockSpec((tm, IN_TN), col(0, 1)), pl.BlockSpec((tm, IN_TN), col(1, 1)),
        pl.BlockSpec((tm, IN_TN), col(2, 2)), pl.BlockSpec((tm, IN_TN), col(4, 2)),
        pl.BlockSpec((tm, GLA_QK), lambda i, j: (i, 0)),
        pl.BlockSpec((tm, FOX_HEADS), lambda i, j: (i, 0)),
        pl.BlockSpec((FOX_HEADS, tm), lambda i, j: (0, i)),
        head_spec(6), head_spec(8), head_spec(10),
        pl.BlockSpec((tm * FOX_HEADS, FOX_HD), lambda i, j: (i, 0)),
        pl.BlockSpec((tm * FOX_HEADS, FOX_HD), lambda i, j: (i, 0)),
    ]
    return pl.pallas_call(
        functools.partial(_in_proj_kernel, nb=nb, lt=lt),
        grid=(m // tm, nj),
        in_specs=[
            pl.BlockSpec((tm, D_MODEL), lambda i, j: (i, 0)),
            pl.BlockSpec((1, D_MODEL), const),
            pl.BlockSpec((D_MODEL, IN_TN), lambda i, j: (0, j)),
            pl.BlockSpec((D_MODEL, LANES), const),
            pl.BlockSpec((16, D_MODEL), const),
            pl.BlockSpec((LANES, GLA_QK), const),
            pl.BlockSpec((1, GLA_QK), const),
            pl.BlockSpec((1, FOX_HEADS), const),
            pl.BlockSpec((FOX_HEADS, 1), const),
        ],
        out_specs=out_specs,
        out_shape=out_shape,
        scratch_shapes=[pltpu.VMEM((tm, D_MODEL), BF16)],
        compiler_params=pltpu.CompilerParams(
            dimension_semantics=("arbitrary", "arbitrary"), vmem_limit_bytes=VMEM_LIMIT),
        name="in_proj",
    )(x, wts["g_pre"], wts["w_main"], wts["w_small"], wts["w_flt"], wts["w_gu"],
      wts["b_gate"], wts["b_f"], wts["b_ft"])


def _gla_kernel(*refs, chunk, nchunks, has_s0):
    if has_s0:
        q_ref, k_ref, v_ref, g_ref, r_ref, gon_ref, s0_ref, o_ref, sout_ref, st = refs
    else:
        q_ref, k_ref, v_ref, g_ref, r_ref, gon_ref, o_ref, sout_ref, st = refs
    t = pl.program_id(2)

    @pl.when(t == 0)
    def _():
        if has_s0:
            st[...] = s0_ref[...].T
        else:
            st[...] = jnp.zeros_like(st)

    row = lax.broadcasted_iota(jnp.int32, (chunk, chunk), 0)
    col = lax.broadcasted_iota(jnp.int32, (chunk, chunk), 1)
    causal = col <= row
    tri = causal.astype(F32)
    qscale = GLA_DK ** -0.5
    gon = gon_ref[...]
    for c in range(nchunks):
        sl = pl.ds(c * chunk, chunk)
        b = jnp.dot(tri, g_ref[sl, :], precision=HIGHEST, preferred_element_type=F32)
        b_last = b[chunk - 1:chunk, :]
        k = k_ref[sl, :]
        q_i = (q_ref[sl, :] * qscale * jnp.exp(b)).astype(BF16)
        k_i = (k * jnp.exp(-b)).astype(BF16)
        k_e = (k * jnp.exp(b_last - b)).astype(BF16)
        v = v_ref[sl, :]
        att = lax.dot_general(q_i, k_i, NT_DIMS, preferred_element_type=F32)
        att = jnp.where(causal, att, 0.0).astype(BF16)
        s_old = st[...]
        o = jnp.dot(att, v, preferred_element_type=F32)
        o = o + lax.dot_general(q_i, s_old.astype(BF16), NT_DIMS, preferred_element_type=F32)
        kv_t = lax.dot_general(v, k_e, TN_DIMS, preferred_element_type=F32)
        st[...] = s_old * jnp.exp(b_last) + kv_t
        r = r_ref[sl, :]
        o_ref[sl, :] = (_rms(o) * gon * (r * jax.nn.sigmoid(r))).astype(o_ref.dtype)

    @pl.when(t == pl.num_programs(2) - 1)
    def _():
        sout_ref[...] = st[...].T


def _gla(qg, kg, vg, gate, rg, g_onorm, s0, batch, seq, rows, chunk):
    has_s0 = s0 is not None
    nchunks = rows // chunk
    r3 = lambda a: a.reshape(batch, seq, a.shape[-1])
    tok = lambda w: pl.BlockSpec((None, rows, w), lambda b, h, t: (b, t, h))
    state_spec = pl.BlockSpec((None, None, GLA_DK, GLA_DV), lambda b, h, t: (b, h, 0, 0))
    in_specs = [tok(GLA_DK), tok(GLA_DK), tok(GLA_DV), tok(GLA_DK), tok(GLA_DV),
                pl.BlockSpec((1, GLA_DV), lambda b, h, t: (0, h))]
    args = [r3(qg), r3(kg), r3(vg), r3(gate), r3(rg), g_onorm]
    if has_s0:
        in_specs.append(state_spec)
        args.append(s0)
    return pl.pallas_call(
        functools.partial(_gla_kernel, chunk=chunk, nchunks=nchunks, has_s0=has_s0),
        grid=(batch, GLA_HEADS, seq // rows),
        in_specs=in_specs,
        out_specs=[tok(GLA_DV), state_spec],
        out_shape=[jax.ShapeDtypeStruct((batch, seq, GLA_WIDTH), BF16),
                   jax.ShapeDtypeStruct((batch, GLA_HEADS, GLA_DK, GLA_DV), F32)],
        scratch_shapes=[pltpu.VMEM((GLA_DV, GLA_DK), F32)],
        compiler_params=pltpu.CompilerParams(
            dimension_semantics=("arbitrary", "arbitrary", "arbitrary"), vmem_limit_bytes=VMEM_LIMIT),
        name="gla",
    )(*args)


def _cumsum_kernel(x_ref, o_ref, *, nblk):
    x = x_ref[...]
    length = nblk * LANES
    r = lax.broadcasted_iota(jnp.int32, (LANES, LANES), 0)
    c = lax.broadcasted_iota(jnp.int32, (LANES, LANES), 1)
    triu = (r <= c).astype(F32)
    li = lax.broadcasted_iota(jnp.int32, (length, LANES), 0)
    ji = lax.broadcasted_iota(jnp.int32, (length, LANES), 1)
    before = (li < ji * LANES).astype(F32)
    offs = jnp.dot(x, before, precision=HIGHEST, preferred_element_type=F32)
    for j in range(nblk):
        blk = x[:, j * LANES:(j + 1) * LANES]
        o_ref[:, j * LANES:(j + 1) * LANES] = (
            jnp.dot(blk, triu, precision=HIGHEST, preferred_element_type=F32) + offs[:, j:j + 1])


def _cumsum(xt, nseg, seglen):
    return pl.pallas_call(
        functools.partial(_cumsum_kernel, nblk=seglen // LANES),
        grid=(nseg,),
        in_specs=[pl.BlockSpec((FOX_HEADS, seglen), lambda s: (0, s))],
        out_specs=pl.BlockSpec((FOX_HEADS, seglen), lambda s: (0, s)),
        out_shape=jax.ShapeDtypeStruct(xt.shape, F32),
        compiler_params=pltpu.CompilerParams(dimension_semantics=("arbitrary",)),
        name="cumsum",
    )(xt)


def _pick_row(block, h):
    sel = lax.broadcasted_iota(jnp.int32, (FOX_HEADS, 1), 0) == h
    return jnp.sum(jnp.where(sel, block, 0.0), axis=0, keepdims=True)


def _pick_col(block, h):
    sel = lax.broadcasted_iota(jnp.int32, (1, FOX_HEADS), 1) == h
    return jnp.sum(jnp.where(sel, block, 0.0), axis=1, keepdims=True)


def _fox_prompt_kernel(q_ref, k_ref, v_ref, crow_ref, ccol_ref, o_ref, s_scr, *, tq, nq):
    h = pl.program_id(1)
    crow = _pick_row(crow_ref[...], h)
    ccol = _pick_col(ccol_ref[...], h)
    row = lax.broadcasted_iota(jnp.int32, (tq, tq), 0)
    col = lax.broadcasted_iota(jnp.int32, (tq, tq), 1)
    causal = col <= row
    blk = lambda j: slice(j * tq, (j + 1) * tq)
    for qi in range(nq):
        q = q_ref[blk(qi), :]
        cq = ccol[blk(qi), :]
        mx = None
        for j in range(qi + 1):
            s = lax.dot_general(q, k_ref[blk(j), :], NT_DIMS, preferred_element_type=F32)
            s = s + (cq - crow[:, blk(j)])
            if j == qi:
                s = jnp.where(causal, s, NEG)
            s_scr[qi, j] = s
            part = _lane_fold(s, jnp.maximum)
            mx = part if mx is None else jnp.maximum(mx, part)
        m = jnp.max(mx, axis=1, keepdims=True)
        lsum = acc = None
        for j in range(qi + 1):
            p = jnp.exp(s_scr[qi, j] - m)
            pv = jnp.dot(p.astype(BF16), v_ref[blk(j), :], preferred_element_type=F32)
            part = _lane_fold(p, jnp.add)
            lsum = part if lsum is None else lsum + part
            acc = pv if acc is None else acc + pv
        o_ref[blk(qi), :] = (acc / jnp.sum(lsum, axis=1, keepdims=True)).astype(o_ref.dtype)


def _fox_prompt(qh, kh, vh, c_row, c_col, batch, seq, tq):
    nq = seq // tq
    whole = pl.BlockSpec((None, None, seq, FOX_HD), lambda b, h: (b, h, 0, 0))
    return pl.pallas_call(
        functools.partial(_fox_prompt_kernel, tq=tq, nq=nq),
        grid=(batch, FOX_HEADS),
        in_specs=[
            whole, whole, whole,
            pl.BlockSpec((FOX_HEADS, seq), lambda b, h: (0, b)),
            pl.BlockSpec((None, seq, FOX_HEADS), lambda b, h: (b, 0, 0)),
        ],
        out_specs=pl.BlockSpec((None, seq, FOX_HD), lambda b, h: (b, 0, h)),
        out_shape=jax.ShapeDtypeStruct((batch, seq, FOX_WIDTH), BF16),
        scratch_shapes=[pltpu.VMEM((nq, nq, tq, tq), F32)],
        compiler_params=pltpu.CompilerParams(
            dimension_semantics=("arbitrary", "arbitrary"), vmem_limit_bytes=VMEM_LIMIT),
        name="fox_prompt",
    )(qh, kh, vh, c_row, c_col)


FOX_TKC = 512


def _fox_sample_kernel(q_ref, kc_ref, vc_ref, kn_ref, vn_ref, cc_ref, cn_ref, cq_ref, o_ref,
                       m_scr, l_scr, acc_scr, *, new):
    c = pl.program_id(1)

    @pl.when(c == 0)
    def _():
        m_scr[...] = jnp.full_like(m_scr, NEG)
        l_scr[...] = jnp.zeros_like(l_scr)
        acc_scr[...] = jnp.zeros_like(acc_scr)

    q = q_ref[...]
    cq = cq_ref[...]

    def logits(k, ck):
        s = jnp.einsum('hqd,hkd->hqk', q, k, preferred_element_type=F32)
        return s + (cq - ck)

    def update(s, v):
        m_old = m_scr[...]
        m_new = jnp.maximum(m_old, jnp.max(s, axis=2, keepdims=True))
        alpha = jnp.exp(m_old - m_new)
        p = jnp.exp(s - m_new)
        l_scr[...] = alpha * l_scr[...] + jnp.sum(p, axis=2, keepdims=True)
        pv = jnp.einsum('hqk,hkd->hqd', p.astype(BF16), v, preferred_element_type=F32)
        acc_scr[...] = alpha * acc_scr[...] + pv
        m_scr[...] = m_new

    def head_rows(ref):
        return jnp.stack([ref[pl.ds(h, FOX_TKC, stride=FOX_HEADS), :]
                          for h in range(FOX_HEADS)]).astype(BF16)

    update(logits(head_rows(kc_ref), cc_ref[...]), head_rows(vc_ref))

    @pl.when(c == pl.num_programs(1) - 1)
    def _():
        row = lax.broadcasted_iota(jnp.int32, (new, new), 0)
        col = lax.broadcasted_iota(jnp.int32, (new, new), 1)
        s = logits(kn_ref[...], cn_ref[:, :, :new])
        update(jnp.where((col <= row)[None], s, NEG), vn_ref[...])
        out = acc_scr[...] / l_scr[...]
        for h in range(FOX_HEADS):
            o_ref[:, h * FOX_HD:(h + 1) * FOX_HD] = out[h].astype(o_ref.dtype)


def _fox_sample(qh, k_hist, v_hist, layer, kh, vh, c_row, c_q, batch, new):
    past = k_hist.shape[2] // FOX_HEADS
    nkc = past // FOX_TKC
    blocks_per_seq = nkc + 1
    heads_new = pl.BlockSpec((None, FOX_HEADS, new, FOX_HD), lambda b, c: (b, 0, 0, 0))
    hist = pl.BlockSpec((None, None, FOX_TKC * FOX_HEADS, FOX_HD), lambda b, c: (layer, b, c, 0))
    return pl.pallas_call(
        functools.partial(_fox_sample_kernel, new=new),
        grid=(batch, nkc),
        in_specs=[heads_new, hist, hist, heads_new, heads_new,
                  pl.BlockSpec((FOX_HEADS, 1, FOX_TKC), lambda b, c: (0, 0, b * blocks_per_seq + c)),
                  pl.BlockSpec((FOX_HEADS, 1, FOX_TKC), lambda b, c: (0, 0, b * blocks_per_seq + nkc)),
                  pl.BlockSpec((None, FOX_HEADS, new, 1), lambda b, c: (b, 0, 0, 0))],
        out_specs=pl.BlockSpec((None, new, FOX_WIDTH), lambda b, c: (b, 0, 0)),
        out_shape=jax.ShapeDtypeStruct((batch, new, FOX_WIDTH), BF16),
        scratch_shapes=[pltpu.VMEM((FOX_HEADS, new, 1), F32),
                        pltpu.VMEM((FOX_HEADS, new, 1), F32),
                        pltpu.VMEM((FOX_HEADS, new, FOX_HD), F32)],
        compiler_params=pltpu.CompilerParams(
            dimension_semantics=("arbitrary", "arbitrary"), vmem_limit_bytes=VMEM_LIMIT),
        name="fox_sample",
    )(qh, k_hist, v_hist, kh, vh, c_row[:, None, :], c_row[:, None, :], c_q)


def _out_proj_kernel(og_ref, of_ref, w_ref, x_ref, gpost_ref, gpre_ref, x1_ref, h2_ref):
    y = jnp.dot(og_ref[...], w_ref[:GLA_WIDTH, :], preferred_element_type=F32)
    y = y + jnp.dot(of_ref[...], w_ref[GLA_WIDTH:, :], preferred_element_type=F32)
    x1 = x_ref[...] + _rms(y) * gpost_ref[...]
    x1_ref[...] = x1
    h2_ref[...] = (_rms(x1) * gpre_ref[...]).astype(BF16)


def _out_proj(o_gla, o_fox, w_out, x, g_post, g_mlp_pre, tm):
    m = x.shape[0]
    const = lambda i: (0, 0)
    rowblk = lambda w: pl.BlockSpec((tm, w), lambda i: (i, 0))
    return pl.pallas_call(
        _out_proj_kernel,
        grid=(m // tm,),
        in_specs=[rowblk(GLA_WIDTH), rowblk(FOX_WIDTH),
                  pl.BlockSpec((D_MODEL, D_MODEL), const),
                  rowblk(D_MODEL),
                  pl.BlockSpec((1, D_MODEL), const), pl.BlockSpec((1, D_MODEL), const)],
        out_specs=[rowblk(D_MODEL), rowblk(D_MODEL)],
        out_shape=[jax.ShapeDtypeStruct((m, D_MODEL), F32), jax.ShapeDtypeStruct((m, D_MODEL), BF16)],
        compiler_params=pltpu.CompilerParams(
            dimension_semantics=("arbitrary",), vmem_limit_bytes=VMEM_LIMIT),
        name="out_proj",
    )(o_gla.reshape(m, GLA_WIDTH), o_fox.reshape(m, FOX_WIDTH), w_out, x, g_post, g_mlp_pre)


MLP_TF = 1024
MLP_TN = 512


def _mlp_kernel(h_ref, wu_ref, wd_ref, x1_ref, g_ref, o_ref, u_scr):
    j = pl.program_id(1)

    @pl.when(j == 0)
    def _():
        o_ref[...] = jnp.zeros_like(o_ref)

    u = jnp.maximum(jnp.dot(h_ref[...], wu_ref[...], preferred_element_type=F32), 0.0)
    u_scr[...] = (u * u).astype(BF16)
    for n in range(0, D_MODEL, MLP_TN):
        o_ref[:, n:n + MLP_TN] += jnp.dot(u_scr[...], wd_ref[:, n:n + MLP_TN],
                                           preferred_element_type=F32)

    @pl.when(j == pl.num_programs(1) - 1)
    def _():
        o_ref[...] = x1_ref[...] + _rms(o_ref[...]) * g_ref[...]


def _mlp(h2, w_up, w_down, x1, g_post, tm):
    m = x1.shape[0]
    return pl.pallas_call(
        _mlp_kernel,
        grid=(m // tm, D_FF // MLP_TF),
        in_specs=[pl.BlockSpec((tm, D_MODEL), lambda i, j: (i, 0)),
                  pl.BlockSpec((D_MODEL, MLP_TF), lambda i, j: (0, j)),
                  pl.BlockSpec((MLP_TF, D_MODEL), lambda i, j: (j, 0)),
                  pl.BlockSpec((tm, D_MODEL), lambda i, j: (i, 0)),
                  pl.BlockSpec((1, D_MODEL), lambda i, j: (0, 0))],
        out_specs=pl.BlockSpec((tm, D_MODEL), lambda i, j: (i, 0)),
        out_shape=jax.ShapeDtypeStruct((m, D_MODEL), F32),
        scratch_shapes=[pltpu.VMEM((tm, MLP_TF), BF16)],
        compiler_params=pltpu.CompilerParams(
            dimension_semantics=("arbitrary", "arbitrary"), vmem_limit_bytes=VMEM_LIMIT),
        name="mlp",
    )(h2, w_up, w_down, x1, g_post)


def _layer_weights(l, g_mix_pre, w_in, w_gla_gate_up, b_gla_gate, b_fox_f, g_gla_onorm, w_out,
                   g_mix_post, g_mlp_pre, w_mlp_up, w_mlp_down, g_mlp_post):
    w = w_in[l]
    w_fl = w[:, O_FL:]
    row = lambda v: v.reshape(1, -1)
    return dict(
        g_pre=row(g_mix_pre[l]),
        w_main=jnp.concatenate([w[:, :O_GLR], w[:, O_RG:O_FL]], axis=1).astype(BF16),
        w_small=jnp.pad(jnp.concatenate([w[:, O_GLR:O_RG], w_fl], axis=1),
                        ((0, 0), (0, LANES - GLA_RANK - FOX_HEADS))).astype(BF16),
        w_flt=jnp.pad(w_fl.T, ((0, 16 - FOX_HEADS), (0, 0))).astype(BF16),
        w_gu=jnp.pad(w_gla_gate_up[l], ((0, LANES - GLA_RANK), (0, 0))).astype(BF16),
        b_gate=row(b_gla_gate[l]),
        b_f=row(b_fox_f[l]),
        b_ft=b_fox_f[l].reshape(-1, 1),
        g_onorm=row(g_gla_onorm[l]),
        w_out=w_out[l].astype(BF16),
        g_post=row(g_mix_post[l]),
        g_mlp_pre=row(g_mlp_pre[l]),
        w_up=w_mlp_up[l].astype(BF16),
        w_down=w_mlp_down[l].astype(BF16),
        g_mlp_post=row(g_mlp_post[l]),
    )


def _layer(x, wts, batch, seq, tm, gla_rows, gla_chunk, s0=None, cache=None):
    qg, kg, vg, rg, gate, logf, logft, qh, kh, vh, kf, vf = _in_proj(x, wts, batch, seq, tm)
    o_gla, s_new = _gla(qg, kg, vg, gate, rg, wts["g_onorm"], s0, batch, seq, gla_rows, gla_chunk)
    if cache is None:
        c_row = _cumsum(logft, batch, seq)
        c_col = c_row.T.reshape(batch, seq, FOX_HEADS)
        o_fox = _fox_prompt(qh, kh, vh, c_row, c_col, batch, seq, tq=512)
    else:
        k_hist, v_hist, logf_cache, layer = cache
        past = logf_cache.shape[1]
        total = past + seq
        seg = past + FOX_TKC
        lf = jnp.concatenate([logf_cache, logf.reshape(batch, seq, FOX_HEADS)], axis=1)
        lf = jnp.pad(lf.transpose(2, 0, 1), ((0, 0), (0, 0), (0, seg - total)))
        c_row = _cumsum(lf.reshape(FOX_HEADS, batch * seg), batch, seg)
        c_q = c_row.reshape(FOX_HEADS, batch, seg)[:, :, past:total].transpose(1, 0, 2)[..., None]
        o_fox = _fox_sample(qh, k_hist, v_hist, layer, kh, vh, c_row, c_q, batch, seq)
    x1, h2 = _out_proj(o_gla, o_fox, wts["w_out"], x, wts["g_post"], wts["g_mlp_pre"], min(tm, 256))
    x2 = _mlp(h2, wts["w_up"], wts["w_down"], x1, wts["g_mlp_post"], tm)
    shp = (batch, seq, FOX_HEADS, FOX_HD)
    return x2, kf.reshape(shp), vf.reshape(shp), logf.reshape(batch, seq, FOX_HEADS), s_new


def kernel(x_prompt, x_sample, cache_fox_k, cache_fox_v, cache_fox_logf, state_gla, g_mix_pre, w_in,
           w_gla_gate_up, b_gla_gate, b_fox_f, g_gla_onorm, w_out, g_mix_post, g_mlp_pre, w_mlp_up,
           w_mlp_down, g_mlp_post):
    pb, pl_, _ = x_prompt.shape
    sb, sl, _ = x_sample.shape
    yp = x_prompt.reshape(pb * pl_, D_MODEL)
    ys = x_sample.reshape(sb * sl, D_MODEL)
    prompt_outs, sample_outs = [], []
    past = cache_fox_k.shape[2]
    k_hist = cache_fox_k.reshape(DEPTH, sb, past * FOX_HEADS, FOX_HD)
    v_hist = cache_fox_v.reshape(DEPTH, sb, past * FOX_HEADS, FOX_HD)
    for l in range(DEPTH):
        wts = _layer_weights(l, g_mix_pre, w_in, w_gla_gate_up, b_gla_gate, b_fox_f, g_gla_onorm,
                             w_out, g_mix_post, g_mlp_pre, w_mlp_up, w_mlp_down, g_mlp_post)
        yp, *outs = _layer(yp, wts, pb, pl_, tm=512, gla_rows=512, gla_chunk=CHUNK)
        prompt_outs.append(outs)
        cache = (k_hist, v_hist, cache_fox_logf[l], l)
        ys, *outs = _layer(ys, wts, sb, sl, tm=sb * sl, gla_rows=sl, gla_chunk=sl,
                           s0=state_gla[l], cache=cache)
        sample_outs.append(outs)
    stack = lambda outs, i: jnp.stack([o[i] for o in outs])
    return (yp.reshape(x_prompt.shape), ys.reshape(x_sample.shape),
            stack(prompt_outs, 0), stack(prompt_outs, 1), stack(prompt_outs, 2), stack(prompt_outs, 3),
            stack(sample_outs, 0), stack(sample_outs, 1), stack(sample_outs, 2), stack(sample_outs, 3))
```

```python
import functools

import jax
import jax.numpy as jnp
from jax import lax
from jax.experimental import pallas as pl
from jax.experimental.pallas import tpu as pltpu

F32 = jnp.float32
BF16 = jnp.bfloat16
HIGHEST = lax.Precision.HIGHEST

D_MODEL = 2048
DEPTH = 2
CHUNK = 64
GLA_HEADS = 4
GLA_DK = 128
GLA_DV = 256
GLA_QK = GLA_HEADS * GLA_DK
GLA_WIDTH = GLA_HEADS * GLA_DV
GLA_RANK = 16
GLA_TAU = 16.0
FOX_HEADS = 8
FOX_HD = 128
FOX_WIDTH = FOX_HEADS * FOX_HD
D_FF = 4 * D_MODEL
EPS = 1e-6
NEG = -1e30

O_GLR = 2 * GLA_QK + GLA_WIDTH
O_RG = O_GLR + GLA_RANK
O_FL = O_RG + GLA_WIDTH + 3 * FOX_WIDTH
PROJ_WIDTH = O_FL + FOX_HEADS
MAIN_WIDTH = PROJ_WIDTH - GLA_RANK - FOX_HEADS
LANES = 128
SMALL_FL = GLA_RANK

VMEM_LIMIT = 52 * 1024 * 1024
NT_DIMS = (((1,), (1,)), ((), ()))
TN_DIMS = (((0,), (0,)), ((), ()))


def _log_sigmoid(x):
    return jnp.minimum(x, 0.0) - jnp.log1p(jnp.exp(-jnp.abs(x)))


def _rms(x):
    return x * lax.rsqrt(jnp.mean(x * x, axis=-1, keepdims=True) + EPS)


def _lane_fold(x, op):
    parts = [x[:, i * LANES:(i + 1) * LANES] for i in range(x.shape[1] // LANES)]
    while len(parts) > 1:
        parts = [op(parts[i], parts[i + 1]) for i in range(0, len(parts), 2)]
    return parts[0]


def _norm_kernel(x_ref, g_ref, h_ref):
    h_ref[...] = (_rms(x_ref[...]) * g_ref[...]).astype(BF16)


def _norm(x, g, tm):
    m = x.shape[0]
    return pl.pallas_call(
        _norm_kernel,
        grid=(m // tm,),
        in_specs=[pl.BlockSpec((tm, D_MODEL), lambda i: (i, 0)),
                  pl.BlockSpec((1, D_MODEL), lambda i: (0, 0))],
        out_specs=pl.BlockSpec((tm, D_MODEL), lambda i: (i, 0)),
        out_shape=jax.ShapeDtypeStruct((m, D_MODEL), BF16),
        compiler_params=pltpu.CompilerParams(
            dimension_semantics=("arbitrary",), vmem_limit_bytes=VMEM_LIMIT),
        name="norm",
    )(x, g)


IN_TN = 512
HEADS_PER_TN = IN_TN // FOX_HD


def _in_proj_kernel(*refs, nb, lt, aliased):
    h_ref, wm_ref, ws_ref, wgu_ref, bg_ref, bf_ref, bft_ref = refs[:7]
    outs = refs[9:] if aliased else refs[7:]
    (qg_ref, kg_ref, vg_ref, rg_ref, gate_ref, logf_ref, logft_ref,
     qh_ref, kh_ref, vh_ref, kf_ref, vf_ref) = outs
    tm = nb * lt
    h = h_ref[...]

    zs = jnp.dot(h, ws_ref[...], preferred_element_type=F32)
    gpre = jnp.dot(zs.astype(BF16), wgu_ref[...], preferred_element_type=F32) + bg_ref[...]
    gate_ref[...] = _log_sigmoid(gpre) / GLA_TAU
    logf_ref[...] = _log_sigmoid(zs[:, SMALL_FL:SMALL_FL + FOX_HEADS] + bf_ref[...])
    logft_ref[...] = _log_sigmoid(zs.T[SMALL_FL:SMALL_FL + FOX_HEADS, :] + bft_ref[...])

    def mm(cb):
        return jnp.dot(h, wm_ref[:, cb * IN_TN:(cb + 1) * IN_TN], preferred_element_type=F32)

    def heads(ref, val, half):
        for hh in range(HEADS_PER_TN):
            ref[:, half * HEADS_PER_TN + hh] = (
                val[:, hh * FOX_HD:(hh + 1) * FOX_HD].astype(BF16).reshape(nb, lt, FOX_HD))

    def rows(ref, val, half):
        for hh in range(HEADS_PER_TN):
            ref[pl.ds(half * HEADS_PER_TN + hh, tm, stride=FOX_HEADS), :] = (
                val[:, hh * FOX_HD:(hh + 1) * FOX_HD])

    qg_ref[...] = mm(0)
    kg_ref[...] = mm(1)
    for half in range(2):
        cols = slice(half * IN_TN, (half + 1) * IN_TN)
        vg_ref[:, cols] = mm(2 + half).astype(BF16)
        rg_ref[:, cols] = mm(4 + half)
        heads(qh_ref, mm(6 + half) * (FOX_HD ** -0.5), half)
        val = mm(8 + half)
        heads(kh_ref, val, half)
        rows(kf_ref, val, half)
        val = mm(10 + half)
        heads(vh_ref, val, half)
        rows(vf_ref, val, half)


def _in_proj(h, wts, batch, seq, tm, layer, kv_prev):
    m = h.shape[0]
    lt = min(tm, seq)
    nb = tm // lt
    nt = seq // lt
    aliased = kv_prev is not None
    const = lambda i: (0, 0)
    rowblk = lambda w: pl.BlockSpec((tm, w), lambda i: (i, 0))
    head_shape = jax.ShapeDtypeStruct((batch, FOX_HEADS, seq, FOX_HD), BF16)
    head_spec = pl.BlockSpec((nb, FOX_HEADS, lt, FOX_HD), lambda i: (i // nt, 0, i % nt, 0))
    kv_shape = jax.ShapeDtypeStruct((DEPTH, m * FOX_HEADS, FOX_HD), F32)
    kv_spec = pl.BlockSpec((None, tm * FOX_HEADS, FOX_HD), lambda i: (layer, i, 0))
    out_shape = [
        jax.ShapeDtypeStruct((m, GLA_QK), F32), jax.ShapeDtypeStruct((m, GLA_QK), F32),
        jax.ShapeDtypeStruct((m, GLA_WIDTH), BF16), jax.ShapeDtypeStruct((m, GLA_WIDTH), F32),
        jax.ShapeDtypeStruct((m, GLA_QK), F32), jax.ShapeDtypeStruct((m, FOX_HEADS), F32),
        jax.ShapeDtypeStruct((FOX_HEADS, m), F32),
        head_shape, head_shape, head_shape, kv_shape, kv_shape,
    ]
    out_specs = [
        rowblk(GLA_QK), rowblk(GLA_QK), rowblk(GLA_WIDTH), rowblk(GLA_WIDTH),
        rowblk(GLA_QK), rowblk(FOX_HEADS),
        pl.BlockSpec((FOX_HEADS, tm), lambda i: (0, i)),
        head_spec, head_spec, head_spec, kv_spec, kv_spec,
    ]
    in_specs = [
        rowblk(D_MODEL),
        pl.BlockSpec((D_MODEL, MAIN_WIDTH), const),
        pl.BlockSpec((D_MODEL, LANES), const),
        pl.BlockSpec((LANES, GLA_QK), const),
        pl.BlockSpec((1, GLA_QK), const),
        pl.BlockSpec((1, FOX_HEADS), const),
        pl.BlockSpec((FOX_HEADS, 1), const),
    ]
    args = [h, wts["w_main"], wts["w_small"], wts["w_gu"], wts["b_gate"], wts["b_f"], wts["b_ft"]]
    aliases = {}
    if aliased:
        in_specs += [pl.BlockSpec(memory_space=pl.ANY)] * 2
        args += list(kv_prev)
        aliases = {7: 10, 8: 11}
    return pl.pallas_call(
        functools.partial(_in_proj_kernel, nb=nb, lt=lt, aliased=aliased),
        grid=(m // tm,),
        in_specs=in_specs,
        out_specs=out_specs,
        out_shape=out_shape,
        input_output_aliases=aliases,
        compiler_params=pltpu.CompilerParams(
            dimension_semantics=("arbitrary",), vmem_limit_bytes=VMEM_LIMIT),
        name="in_proj",
    )(*args)


def _gla_kernel(*refs, chunk, nchunks, has_s0):
    if has_s0:
        q_ref, k_ref, v_ref, g_ref, r_ref, gon_ref, s0_ref, o_ref, sout_ref, st = refs
    else:
        q_ref, k_ref, v_ref, g_ref, r_ref, gon_ref, o_ref, sout_ref, st = refs
    t = pl.program_id(2)

    @pl.when(t == 0)
    def _():
        if has_s0:
            st[...] = s0_ref[...].T
        else:
            st[...] = jnp.zeros_like(st)

    row = lax.broadcasted_iota(jnp.int32, (chunk, chunk), 0)
    col = lax.broadcasted_iota(jnp.int32, (chunk, chunk), 1)
    causal = col <= row
    tri = causal.astype(F32)
    qscale = GLA_DK ** -0.5
    gon = gon_ref[...]
    chunks = range(nchunks)
    sl = lambda c: pl.ds(c * chunk, chunk)
    bs = [jnp.dot(tri, g_ref[sl(c), :], precision=HIGHEST, preferred_element_type=F32)
          for c in chunks]
    b_lasts = [b[chunk - 1:chunk, :] for b in bs]
    q_is = [(q_ref[sl(c), :] * qscale * jnp.exp(bs[c])).astype(BF16) for c in chunks]
    k_is = [(k_ref[sl(c), :] * jnp.exp(-bs[c])).astype(BF16) for c in chunks]
    k_es = [(k_ref[sl(c), :] * jnp.exp(b_lasts[c] - bs[c])).astype(BF16) for c in chunks]
    atts = [lax.dot_general(q_is[c], k_is[c], NT_DIMS, preferred_element_type=F32) for c in chunks]
    kv_ts = [lax.dot_general(v_ref[sl(c), :], k_es[c], TN_DIMS, preferred_element_type=F32)
             for c in chunks]
    atts = [jnp.where(causal, a, 0.0).astype(BF16) for a in atts]
    o_intra = [jnp.dot(atts[c], v_ref[sl(c), :], preferred_element_type=F32) for c in chunks]
    states = [st[...]]
    for c in chunks:
        states.append(states[c] * jnp.exp(b_lasts[c]) + kv_ts[c])
    o_inter = [lax.dot_general(q_is[c], states[c].astype(BF16), NT_DIMS, preferred_element_type=F32)
               for c in chunks]
    for c in chunks:
        r = r_ref[sl(c), :]
        o = o_intra[c] + o_inter[c]
        o_ref[sl(c), :] = (_rms(o) * gon * (r * jax.nn.sigmoid(r))).astype(o_ref.dtype)
    state = states[nchunks]
    st[...] = state

    @pl.when(t == pl.num_programs(2) - 1)
    def _():
        sout_ref[...] = state.T


def _gla(qg, kg, vg, gate, rg, g_onorm, s0, batch, seq, rows, chunk):
    has_s0 = s0 is not None
    nchunks = rows // chunk
    r3 = lambda a: a.reshape(batch, seq, a.shape[-1])
    tok = lambda w: pl.BlockSpec((None, rows, w), lambda b, h, t: (b, t, h))
    state_spec = pl.BlockSpec((None, None, GLA_DK, GLA_DV), lambda b, h, t: (b, h, 0, 0))
    in_specs = [tok(GLA_DK), tok(GLA_DK), tok(GLA_DV), tok(GLA_DK), tok(GLA_DV),
                pl.BlockSpec((1, GLA_DV), lambda b, h, t: (0, h))]
    args = [r3(qg), r3(kg), r3(vg), r3(gate), r3(rg), g_onorm]
    if has_s0:
        in_specs.append(state_spec)
        args.append(s0)
    return pl.pallas_call(
        functools.partial(_gla_kernel, chunk=chunk, nchunks=nchunks, has_s0=has_s0),
        grid=(batch, GLA_HEADS, seq // rows),
        in_specs=in_specs,
        out_specs=[tok(GLA_DV), state_spec],
        out_shape=[jax.ShapeDtypeStruct((batch, seq, GLA_WIDTH), BF16),
                   jax.ShapeDtypeStruct((batch, GLA_HEADS, GLA_DK, GLA_DV), F32)],
        scratch_shapes=[pltpu.VMEM((GLA_DV, GLA_DK), F32)],
        compiler_params=pltpu.CompilerParams(
            dimension_semantics=("arbitrary", "arbitrary", "arbitrary"), vmem_limit_bytes=VMEM_LIMIT),
        name="gla",
    )(*args)


def _cumsum_kernel(x_ref, o_ref, *, nblk):
    x = x_ref[...]
    length = nblk * LANES
    r = lax.broadcasted_iota(jnp.int32, (LANES, LANES), 0)
    c = lax.broadcasted_iota(jnp.int32, (LANES, LANES), 1)
    triu = (r <= c).astype(F32)
    li = lax.broadcasted_iota(jnp.int32, (length, LANES), 0)
    ji = lax.broadcasted_iota(jnp.int32, (length, LANES), 1)
    before = (li < ji * LANES).astype(F32)
    offs = jnp.dot(x, before, precision=HIGHEST, preferred_element_type=F32)
    for j in range(nblk):
        blk = x[:, j * LANES:(j + 1) * LANES]
        o_ref[:, j * LANES:(j + 1) * LANES] = (
            jnp.dot(blk, triu, precision=HIGHEST, preferred_element_type=F32) + offs[:, j:j + 1])


def _cumsum(xt, nseg, seglen):
    return pl.pallas_call(
        functools.partial(_cumsum_kernel, nblk=seglen // LANES),
        grid=(nseg,),
        in_specs=[pl.BlockSpec((FOX_HEADS, seglen), lambda s: (0, s))],
        out_specs=pl.BlockSpec((FOX_HEADS, seglen), lambda s: (0, s)),
        out_shape=jax.ShapeDtypeStruct(xt.shape, F32),
        compiler_params=pltpu.CompilerParams(dimension_semantics=("arbitrary",)),
        name="cumsum",
    )(xt)


def _pick_row(block, h):
    sel = lax.broadcasted_iota(jnp.int32, (FOX_HEADS, 1), 0) == h
    return jnp.sum(jnp.where(sel, block, 0.0), axis=0, keepdims=True)


def _pick_col(block, h):
    sel = lax.broadcasted_iota(jnp.int32, (1, FOX_HEADS), 1) == h
    return jnp.sum(jnp.where(sel, block, 0.0), axis=1, keepdims=True)


def _fox_prompt_kernel(q_ref, k_ref, v_ref, crow_ref, ccol_ref, o_ref, s_scr, *, tq, nq):
    h = pl.program_id(1)
    crow = _pick_row(crow_ref[...], h)
    ccol = _pick_col(ccol_ref[...], h)
    row = lax.broadcasted_iota(jnp.int32, (tq, tq), 0)
    col = lax.broadcasted_iota(jnp.int32, (tq, tq), 1)
    causal = col <= row
    blk = lambda j: slice(j * tq, (j + 1) * tq)

    def pass1(qi):
        q = q_ref[blk(qi), :]
        cq = ccol[blk(qi), :]
        mx = None
        for j in range(qi + 1):
            s = lax.dot_general(q, k_ref[blk(j), :], NT_DIMS, preferred_element_type=F32)
            s = s + (cq - crow[:, blk(j)])
            if j == qi:
                s = jnp.where(causal, s, NEG)
            s_scr[qi, j] = s
            part = _lane_fold(s, jnp.maximum)
            mx = part if mx is None else jnp.maximum(mx, part)
        return jnp.max(mx, axis=1, keepdims=True)

    def pass2(qi, m):
        lsum = acc = None
        for j in range(qi + 1):
            p = jnp.exp(s_scr[qi, j] - m)
            pv = jnp.dot(p.astype(BF16), v_ref[blk(j), :], preferred_element_type=F32)
            part = _lane_fold(p, jnp.add)
            lsum = part if lsum is None else lsum + part
            acc = pv if acc is None else acc + pv
        o_ref[blk(qi), :] = (acc / jnp.sum(lsum, axis=1, keepdims=True)).astype(o_ref.dtype)

    m_prev = pass1(0)
    for qi in range(1, nq):
        m_next = pass1(qi)
        pass2(qi - 1, m_prev)
        m_prev = m_next
    pass2(nq - 1, m_prev)


def _fox_prompt(qh, kh, vh, c_row, c_col, batch, seq, tq):
    nq = seq // tq
    whole = pl.BlockSpec((None, None, seq, FOX_HD), lambda b, h: (b, h, 0, 0))
    return pl.pallas_call(
        functools.partial(_fox_prompt_kernel, tq=tq, nq=nq),
        grid=(batch, FOX_HEADS),
        in_specs=[
            whole, whole, whole,
            pl.BlockSpec((FOX_HEADS, seq), lambda b, h: (0, b)),
            pl.BlockSpec((None, seq, FOX_HEADS), lambda b, h: (b, 0, 0)),
        ],
        out_specs=pl.BlockSpec((None, seq, FOX_HD), lambda b, h: (b, 0, h)),
        out_shape=jax.ShapeDtypeStruct((batch, seq, FOX_WIDTH), BF16),
        scratch_shapes=[pltpu.VMEM((nq, nq, tq, tq), F32)],
        compiler_params=pltpu.CompilerParams(
            dimension_semantics=("arbitrary", "arbitrary"), vmem_limit_bytes=VMEM_LIMIT),
        name="fox_prompt",
    )(qh, kh, vh, c_row, c_col)


FOX_TKC = 512


def _fox_sample_kernel(q_ref, kc_ref, vc_ref, kn_ref, vn_ref, cc_ref, cn_ref, cq_ref, o_ref,
                       m_scr, l_scr, acc_scr, *, new):
    c = pl.program_id(1)

    @pl.when(c == 0)
    def _():
        m_scr[...] = jnp.full_like(m_scr, NEG)
        l_scr[...] = jnp.zeros_like(l_scr)
        acc_scr[...] = jnp.zeros_like(acc_scr)

    q = q_ref[...]
    cq = cq_ref[...]

    def logits(k, ck):
        s = jnp.einsum('hqd,hkd->hqk', q, k, preferred_element_type=F32)
        return s + (cq - ck)

    def update(s, v):
        m_old = m_scr[...]
        m_new = jnp.maximum(m_old, jnp.max(s, axis=2, keepdims=True))
        alpha = jnp.exp(m_old - m_new)
        p = jnp.exp(s - m_new)
        l_scr[...] = alpha * l_scr[...] + jnp.sum(p, axis=2, keepdims=True)
        pv = jnp.einsum('hqk,hkd->hqd', p.astype(BF16), v, preferred_element_type=F32)
        acc_scr[...] = alpha * acc_scr[...] + pv
        m_scr[...] = m_new

    def head_rows(ref):
        return jnp.stack([ref[pl.ds(h, FOX_TKC, stride=FOX_HEADS), :]
                          for h in range(FOX_HEADS)]).astype(BF16)

    update(logits(head_rows(kc_ref), cc_ref[...]), head_rows(vc_ref))

    @pl.when(c == pl.num_programs(1) - 1)
    def _():
        row = lax.broadcasted_iota(jnp.int32, (new, new), 0)
        col = lax.broadcasted_iota(jnp.int32, (new, new), 1)
        s = logits(kn_ref[...], cn_ref[:, :, :new])
        update(jnp.where((col <= row)[None], s, NEG), vn_ref[...])
        out = acc_scr[...] / l_scr[...]
        for h in range(FOX_HEADS):
            o_ref[:, h * FOX_HD:(h + 1) * FOX_HD] = out[h].astype(o_ref.dtype)


def _fox_sample(qh, k_hist, v_hist, layer, kh, vh, c_row, c_q, batch, new):
    past = k_hist.shape[2] // FOX_HEADS
    nkc = past // FOX_TKC
    blocks_per_seq = nkc + 1
    heads_new = pl.BlockSpec((None, FOX_HEADS, new, FOX_HD), lambda b, c: (b, 0, 0, 0))
    hist = pl.BlockSpec((None, None, FOX_TKC * FOX_HEADS, FOX_HD), lambda b, c: (layer, b, c, 0))
    return pl.pallas_call(
        functools.partial(_fox_sample_kernel, new=new),
        grid=(batch, nkc),
        in_specs=[heads_new, hist, hist, heads_new, heads_new,
                  pl.BlockSpec((FOX_HEADS, 1, FOX_TKC), lambda b, c: (0, 0, b * blocks_per_seq + c)),
                  pl.BlockSpec((FOX_HEADS, 1, FOX_TKC), lambda b, c: (0, 0, b * blocks_per_seq + nkc)),
                  pl.BlockSpec((None, FOX_HEADS, new, 1), lambda b, c: (b, 0, 0, 0))],
        out_specs=pl.BlockSpec((None, new, FOX_WIDTH), lambda b, c: (b, 0, 0)),
        out_shape=jax.ShapeDtypeStruct((batch, new, FOX_WIDTH), BF16),
        scratch_shapes=[pltpu.VMEM((FOX_HEADS, new, 1), F32),
                        pltpu.VMEM((FOX_HEADS, new, 1), F32),
                        pltpu.VMEM((FOX_HEADS, new, FOX_HD), F32)],
        compiler_params=pltpu.CompilerParams(
            dimension_semantics=("arbitrary", "arbitrary"), vmem_limit_bytes=VMEM_LIMIT),
        name="fox_sample",
    )(qh, k_hist, v_hist, kh, vh, c_row[:, None, :], c_row[:, None, :], c_q)


def _out_proj_kernel(og_ref, of_ref, w_ref, x_ref, gpost_ref, gpre_ref, x1_ref, h2_ref):
    y = jnp.dot(og_ref[...], w_ref[:GLA_WIDTH, :], preferred_element_type=F32)
    y = y + jnp.dot(of_ref[...], w_ref[GLA_WIDTH:, :], preferred_element_type=F32)
    x1 = x_ref[...] + _rms(y) * gpost_ref[...]
    x1_ref[...] = x1
    h2_ref[...] = (_rms(x1) * gpre_ref[...]).astype(BF16)


def _out_proj(o_gla, o_fox, w_out, layer, x, g_post, g_mlp_pre, tm):
    m = x.shape[0]
    const = lambda i: (0, 0)
    rowblk = lambda w: pl.BlockSpec((tm, w), lambda i: (i, 0))
    return pl.pallas_call(
        _out_proj_kernel,
        grid=(m // tm,),
        in_specs=[rowblk(GLA_WIDTH), rowblk(FOX_WIDTH),
                  pl.BlockSpec((None, D_MODEL, D_MODEL), lambda i: (layer, 0, 0)),
                  rowblk(D_MODEL),
                  pl.BlockSpec((1, D_MODEL), const), pl.BlockSpec((1, D_MODEL), const)],
        out_specs=[rowblk(D_MODEL), rowblk(D_MODEL)],
        out_shape=[jax.ShapeDtypeStruct((m, D_MODEL), F32), jax.ShapeDtypeStruct((m, D_MODEL), BF16)],
        compiler_params=pltpu.CompilerParams(
            dimension_semantics=("arbitrary",), vmem_limit_bytes=VMEM_LIMIT),
        name="out_proj",
    )(o_gla.reshape(m, GLA_WIDTH), o_fox.reshape(m, FOX_WIDTH), w_out, x, g_post, g_mlp_pre)


MLP_TF = 1024
MLP_TN = 512


def _mlp_kernel(*refs, has_next):
    if has_next:
        h_ref, wu_ref, wd_ref, x1_ref, g_ref, gnext_ref, o_ref, hn_ref, u_scr = refs
    else:
        h_ref, wu_ref, wd_ref, x1_ref, g_ref, o_ref, u_scr = refs
    j = pl.program_id(1)

    @pl.when(j == 0)
    def _():
        o_ref[...] = jnp.zeros_like(o_ref)

    u = jnp.maximum(jnp.dot(h_ref[...], wu_ref[...], preferred_element_type=F32), 0.0)
    u_scr[...] = (u * u).astype(BF16)
    for n in range(0, D_MODEL, MLP_TN):
        o_ref[:, n:n + MLP_TN] += jnp.dot(u_scr[...], wd_ref[:, n:n + MLP_TN],
                                           preferred_element_type=F32)

    @pl.when(j == pl.num_programs(1) - 1)
    def _():
        x2 = x1_ref[...] + _rms(o_ref[...]) * g_ref[...]
        o_ref[...] = x2
        if has_next:
            hn_ref[...] = (_rms(x2) * gnext_ref[...]).astype(BF16)


def _mlp(h2, w_up, w_down, layer, x1, g_post, g_next, tm):
    m = x1.shape[0]
    has_next = g_next is not None
    rowblk = pl.BlockSpec((tm, D_MODEL), lambda i, j: (i, 0))
    gain = pl.BlockSpec((1, D_MODEL), lambda i, j: (0, 0))
    in_specs = [rowblk,
                pl.BlockSpec((None, D_MODEL, MLP_TF), lambda i, j: (layer, 0, j)),
                pl.BlockSpec((None, MLP_TF, D_MODEL), lambda i, j: (layer, j, 0)),
                rowblk, gain]
    args = [h2, w_up, w_down, x1, g_post]
    out_specs = [rowblk]
    out_shape = [jax.ShapeDtypeStruct((m, D_MODEL), F32)]
    if has_next:
        in_specs.append(gain)
        args.append(g_next)
        out_specs.append(rowblk)
        out_shape.append(jax.ShapeDtypeStruct((m, D_MODEL), BF16))
    outs = pl.pallas_call(
        functools.partial(_mlp_kernel, has_next=has_next),
        grid=(m // tm, D_FF // MLP_TF),
        in_specs=in_specs,
        out_specs=out_specs,
        out_shape=out_shape,
        scratch_shapes=[pltpu.VMEM((tm, MLP_TF), BF16)],
        compiler_params=pltpu.CompilerParams(
            dimension_semantics=("arbitrary", "arbitrary"), vmem_limit_bytes=VMEM_LIMIT),
        name="mlp",
    )(*args)
    return (outs[0], outs[1]) if has_next else (outs[0], None)


def _layer_weights(l, g_mix_pre, w_in, w_gla_gate_up, b_gla_gate, b_fox_f, g_gla_onorm,
                   g_mix_post, g_mlp_pre, g_mlp_post):
    row = lambda v: v.reshape(1, -1)
    w_small = jnp.concatenate([w_in[l, :, O_GLR:O_RG], w_in[l, :, O_FL:]], axis=1)
    return dict(
        g_pre=row(g_mix_pre[l]),
        w_main=jnp.concatenate([w_in[l, :, :O_GLR], w_in[l, :, O_RG:O_FL]], axis=1).astype(BF16),
        w_small=jnp.pad(w_small, ((0, 0), (0, LANES - GLA_RANK - FOX_HEADS))).astype(BF16),
        w_gu=jnp.pad(w_gla_gate_up[l], ((0, LANES - GLA_RANK), (0, 0))).astype(BF16),
        b_gate=row(b_gla_gate[l]),
        b_f=row(b_fox_f[l]),
        b_ft=b_fox_f[l].reshape(-1, 1),
        g_onorm=row(g_gla_onorm[l]),
        g_post=row(g_mix_post[l]),
        g_mlp_pre=row(g_mlp_pre[l]),
        g_mlp_post=row(g_mlp_post[l]),
    )


def _layer(x, h, wts, g_next, batch, seq, layer, kv_prev, tm_proj, tm, gla_rows, gla_chunk,
           s0=None, cache=None):
    qg, kg, vg, rg, gate, logf, logft, qh, kh, vh, kf, vf = _in_proj(
        h, wts, batch, seq, tm_proj, layer, kv_prev)
    o_gla, s_new = _gla(qg, kg, vg, gate, rg, wts["g_onorm"], s0, batch, seq, gla_rows, gla_chunk)
    if cache is None:
        c_row = _cumsum(logft, batch, seq)
        c_col = c_row.T.reshape(batch, seq, FOX_HEADS)
        o_fox = _fox_prompt(qh, kh, vh, c_row, c_col, batch, seq, tq=512)
    else:
        k_hist, v_hist, logf_cache = cache
        past = logf_cache.shape[1]
        total = past + seq
        seg = past + FOX_TKC
        lf = jnp.concatenate([logf_cache, logf.reshape(batch, seq, FOX_HEADS)], axis=1)
        lf = jnp.pad(lf.transpose(2, 0, 1), ((0, 0), (0, 0), (0, seg - total)))
        c_row = _cumsum(lf.reshape(FOX_HEADS, batch * seg), batch, seg)
        c_q = c_row.reshape(FOX_HEADS, batch, seg)[:, :, past:total].transpose(1, 0, 2)[..., None]
        o_fox = _fox_sample(qh, k_hist, v_hist, layer, kh, vh, c_row, c_q, batch, seq)
    x1, h2 = _out_proj(o_gla, o_fox, wts["w_out"], layer, x, wts["g_post"], wts["g_mlp_pre"],
                       min(tm, 256))
    x2, h_next = _mlp(h2, wts["w_up"], wts["w_down"], layer, x1, wts["g_mlp_post"], g_next, tm)
    return x2, h_next, (kf, vf), logf.reshape(batch, seq, FOX_HEADS), s_new


def kernel(x_prompt, x_sample, cache_fox_k, cache_fox_v, cache_fox_logf, state_gla, g_mix_pre, w_in,
           w_gla_gate_up, b_gla_gate, b_fox_f, g_gla_onorm, w_out, g_mix_post, g_mlp_pre, w_mlp_up,
           w_mlp_down, g_mlp_post):
    pb, pl_, _ = x_prompt.shape
    sb, sl, _ = x_sample.shape
    yp = x_prompt.reshape(pb * pl_, D_MODEL)
    ys = x_sample.reshape(sb * sl, D_MODEL)
    past = cache_fox_k.shape[2]
    k_hist = cache_fox_k.reshape(DEPTH, sb, past * FOX_HEADS, FOX_HD)
    v_hist = cache_fox_v.reshape(DEPTH, sb, past * FOX_HEADS, FOX_HD)
    g0 = g_mix_pre[0].reshape(1, -1)
    hp = _norm(yp, g0, 512)
    hs = _norm(ys, g0, sb * sl)
    kv_p = kv_s = None
    logf_p, logf_s, state_p, state_s = [], [], [], []
    big = dict(w_out=w_out.astype(BF16), w_up=w_mlp_up.astype(BF16), w_down=w_mlp_down.astype(BF16))
    for l in range(DEPTH):
        wts = _layer_weights(l, g_mix_pre, w_in, w_gla_gate_up, b_gla_gate, b_fox_f, g_gla_onorm,
                             g_mix_post, g_mlp_pre, g_mlp_post)
        wts.update(big)
        g_next = g_mix_pre[l + 1].reshape(1, -1) if l + 1 < DEPTH else None
        yp, hp, kv_p, lf, st = _layer(yp, hp, wts, g_next, pb, pl_, l, kv_p, tm_proj=256, tm=512,
                                      gla_rows=512, gla_chunk=CHUNK)
        logf_p.append(lf)
        state_p.append(st)
        ys, hs, kv_s, lf, st = _layer(ys, hs, wts, g_next, sb, sl, l, kv_s, tm_proj=sb * sl,
                                      tm=sb * sl, gla_rows=sl, gla_chunk=sl, s0=state_gla[l],
                                      cache=(k_hist, v_hist, cache_fox_logf[l]))
        logf_s.append(lf)
        state_s.append(st)
    kv5 = lambda a, b, s: a.reshape(DEPTH, b, s, FOX_HEADS, FOX_HD)
    return (yp.reshape(x_prompt.shape), ys.reshape(x_sample.shape),
            kv5(kv_p[0], pb, pl_), kv5(kv_p[1], pb, pl_), jnp.stack(logf_p), jnp.stack(state_p),
            kv5(kv_s[0], sb, sl), kv5(kv_s[1], sb, sl), jnp.stack(logf_s), jnp.stack(state_s))
```

```python
import functools

import jax
import jax.numpy as jnp
from jax import lax
from jax.experimental import pallas as pl
from jax.experimental.pallas import tpu as pltpu

F32 = jnp.float32
BF16 = jnp.bfloat16
HIGHEST = lax.Precision.HIGHEST

D_MODEL = 2048
DEPTH = 2
CHUNK = 64
GLA_HEADS = 4
GLA_DK = 128
GLA_DV = 256
GLA_QK = GLA_HEADS * GLA_DK
GLA_WIDTH = GLA_HEADS * GLA_DV
GLA_RANK = 16
GLA_TAU = 16.0
FOX_HEADS = 8
FOX_HD = 128
FOX_WIDTH = FOX_HEADS * FOX_HD
D_FF = 4 * D_MODEL
EPS = 1e-6
NEG = -1e30

O_GLR = 2 * GLA_QK + GLA_WIDTH
O_RG = O_GLR + GLA_RANK
O_FL = O_RG + GLA_WIDTH + 3 * FOX_WIDTH
PROJ_WIDTH = O_FL + FOX_HEADS
MAIN_WIDTH = PROJ_WIDTH - GLA_RANK - FOX_HEADS
LANES = 128
SMALL_FL = GLA_RANK

VMEM_LIMIT = 52 * 1024 * 1024
NT_DIMS = (((1,), (1,)), ((), ()))
TN_DIMS = (((0,), (0,)), ((), ()))


def _log_sigmoid(x):
    return jnp.minimum(x, 0.0) - jnp.log1p(jnp.exp(-jnp.abs(x)))


def _rms(x):
    return x * lax.rsqrt(jnp.mean(x * x, axis=-1, keepdims=True) + EPS)


def _lane_fold(x, op):
    parts = [x[:, i * LANES:(i + 1) * LANES] for i in range(x.shape[1] // LANES)]
    while len(parts) > 1:
        parts = [op(parts[i], parts[i + 1]) for i in range(0, len(parts), 2)]
    return parts[0]


def _norm_kernel(x_ref, g_ref, h_ref):
    h_ref[...] = (_rms(x_ref[...]) * g_ref[...]).astype(BF16)


def _norm(x, g, tm):
    m = x.shape[0]
    return pl.pallas_call(
        _norm_kernel,
        grid=(m // tm,),
        in_specs=[pl.BlockSpec((tm, D_MODEL), lambda i: (i, 0)),
                  pl.BlockSpec((1, D_MODEL), lambda i: (0, 0))],
        out_specs=pl.BlockSpec((tm, D_MODEL), lambda i: (i, 0)),
        out_shape=jax.ShapeDtypeStruct((m, D_MODEL), BF16),
        compiler_params=pltpu.CompilerParams(
            dimension_semantics=("arbitrary",), vmem_limit_bytes=VMEM_LIMIT),
        name="norm",
    )(x, g)


IN_TN = 512
HEADS_PER_TN = IN_TN // FOX_HD
GLA_BLOCKS = O_GLR // IN_TN
N_IN = 8


def _in_proj_kernel(*refs, nb, lt, aliased):
    h_ref, wa_ref, wb_ref, ws_ref, wgu_ref, bg_ref, bf_ref, bft_ref = refs[:N_IN]
    outs = refs[N_IN + 2:] if aliased else refs[N_IN:]
    (qg_ref, kg_ref, vg_ref, rg_ref, gate_ref, logf_ref, logft_ref,
     qh_ref, kh_ref, vh_ref, kf_ref, vf_ref) = outs
    tm = nb * lt
    h = h_ref[...]

    zs = jnp.dot(h, ws_ref[...], preferred_element_type=F32)
    gpre = jnp.dot(zs.astype(BF16), wgu_ref[...], preferred_element_type=F32) + bg_ref[...]
    gate_ref[...] = _log_sigmoid(gpre) / GLA_TAU
    logf_ref[...] = _log_sigmoid(zs[:, SMALL_FL:SMALL_FL + FOX_HEADS] + bf_ref[...])
    logft_ref[...] = _log_sigmoid(zs.T[SMALL_FL:SMALL_FL + FOX_HEADS, :] + bft_ref[...])

    def mm(cb):
        ref, cb = (wa_ref, cb) if cb < GLA_BLOCKS else (wb_ref, cb - GLA_BLOCKS)
        return jnp.dot(h, ref[:, cb * IN_TN:(cb + 1) * IN_TN], preferred_element_type=F32)

    def heads(ref, val, half):
        for hh in range(HEADS_PER_TN):
            ref[:, half * HEADS_PER_TN + hh] = (
                val[:, hh * FOX_HD:(hh + 1) * FOX_HD].astype(BF16).reshape(nb, lt, FOX_HD))

    def rows(ref, val, half):
        for hh in range(HEADS_PER_TN):
            ref[pl.ds(half * HEADS_PER_TN + hh, tm, stride=FOX_HEADS), :] = (
                val[:, hh * FOX_HD:(hh + 1) * FOX_HD])

    qg_ref[...] = mm(0)
    kg_ref[...] = mm(1)
    for half in range(2):
        cols = slice(half * IN_TN, (half + 1) * IN_TN)
        vg_ref[:, cols] = mm(2 + half).astype(BF16)
        rg_ref[:, cols] = mm(4 + half)
        heads(qh_ref, mm(6 + half) * (FOX_HD ** -0.5), half)
        val = mm(8 + half)
        heads(kh_ref, val, half)
        rows(kf_ref, val, half)
        val = mm(10 + half)
        heads(vh_ref, val, half)
        rows(vf_ref, val, half)


def _in_proj(h, wts, batch, seq, tm, layer, kv_prev):
    m = h.shape[0]
    lt = min(tm, seq)
    nb = tm // lt
    nt = seq // lt
    aliased = kv_prev is not None
    const = lambda i: (0, 0)
    rowblk = lambda w: pl.BlockSpec((tm, w), lambda i: (i, 0))
    head_shape = jax.ShapeDtypeStruct((batch, FOX_HEADS, seq, FOX_HD), BF16)
    head_spec = pl.BlockSpec((nb, FOX_HEADS, lt, FOX_HD), lambda i: (i // nt, 0, i % nt, 0))
    kv_shape = jax.ShapeDtypeStruct((DEPTH, m * FOX_HEADS, FOX_HD), F32)
    kv_spec = pl.BlockSpec((None, tm * FOX_HEADS, FOX_HD), lambda i: (layer, i, 0))
    out_shape = [
        jax.ShapeDtypeStruct((m, GLA_QK), F32), jax.ShapeDtypeStruct((m, GLA_QK), F32),
        jax.ShapeDtypeStruct((m, GLA_WIDTH), BF16), jax.ShapeDtypeStruct((m, GLA_WIDTH), F32),
        jax.ShapeDtypeStruct((m, GLA_QK), F32), jax.ShapeDtypeStruct((m, FOX_HEADS), F32),
        jax.ShapeDtypeStruct((FOX_HEADS, m), F32),
        head_shape, head_shape, head_shape, kv_shape, kv_shape,
    ]
    out_specs = [
        rowblk(GLA_QK), rowblk(GLA_QK), rowblk(GLA_WIDTH), rowblk(GLA_WIDTH),
        rowblk(GLA_QK), rowblk(FOX_HEADS),
        pl.BlockSpec((FOX_HEADS, tm), lambda i: (0, i)),
        head_spec, head_spec, head_spec, kv_spec, kv_spec,
    ]
    in_specs = [
        rowblk(D_MODEL),
        pl.BlockSpec((D_MODEL, O_GLR), const),
        pl.BlockSpec((D_MODEL, MAIN_WIDTH - O_GLR), const),
        pl.BlockSpec((D_MODEL, LANES), const),
        pl.BlockSpec((LANES, GLA_QK), const),
        pl.BlockSpec((1, GLA_QK), const),
        pl.BlockSpec((1, FOX_HEADS), const),
        pl.BlockSpec((FOX_HEADS, 1), const),
    ]
    args = [h, wts["w_a"], wts["w_b"], wts["w_small"], wts["w_gu"], wts["b_gate"], wts["b_f"],
            wts["b_ft"]]
    assert len(args) == N_IN
    aliases = {}
    if aliased:
        in_specs += [pl.BlockSpec(memory_space=pl.ANY)] * 2
        args += list(kv_prev)
        aliases = {N_IN: 10, N_IN + 1: 11}
    return pl.pallas_call(
        functools.partial(_in_proj_kernel, nb=nb, lt=lt, aliased=aliased),
        grid=(m // tm,),
        in_specs=in_specs,
        out_specs=out_specs,
        out_shape=out_shape,
        input_output_aliases=aliases,
        compiler_params=pltpu.CompilerParams(
            dimension_semantics=("arbitrary",), vmem_limit_bytes=VMEM_LIMIT),
        name="in_proj",
    )(*args)


def _gla_kernel(*refs, chunk, nchunks, has_s0):
    if has_s0:
        q_ref, k_ref, v_ref, g_ref, r_ref, gon_ref, s0_ref, o_ref, sout_ref, st = refs
    else:
        q_ref, k_ref, v_ref, g_ref, r_ref, gon_ref, o_ref, sout_ref, st = refs
    t = pl.program_id(2)

    @pl.when(t == 0)
    def _():
        if has_s0:
            st[...] = s0_ref[...].T
        else:
            st[...] = jnp.zeros_like(st)

    row = lax.broadcasted_iota(jnp.int32, (chunk, chunk), 0)
    col = lax.broadcasted_iota(jnp.int32, (chunk, chunk), 1)
    causal = col <= row
    tri = causal.astype(F32)
    qscale = GLA_DK ** -0.5
    gon = gon_ref[...]
    chunks = range(nchunks)
    sl = lambda c: pl.ds(c * chunk, chunk)
    bs = [jnp.dot(tri, g_ref[sl(c), :], precision=HIGHEST, preferred_element_type=F32)
          for c in chunks]
    b_lasts = [b[chunk - 1:chunk, :] for b in bs]
    q_is = [(q_ref[sl(c), :] * qscale * jnp.exp(bs[c])).astype(BF16) for c in chunks]
    k_is = [(k_ref[sl(c), :] * jnp.exp(-bs[c])).astype(BF16) for c in chunks]
    k_es = [(k_ref[sl(c), :] * jnp.exp(b_lasts[c] - bs[c])).astype(BF16) for c in chunks]
    atts = [lax.dot_general(q_is[c], k_is[c], NT_DIMS, preferred_element_type=F32) for c in chunks]
    kv_ts = [lax.dot_general(v_ref[sl(c), :], k_es[c], TN_DIMS, preferred_element_type=F32)
             for c in chunks]
    atts = [jnp.where(causal, a, 0.0).astype(BF16) for a in atts]
    o_intra = [jnp.dot(atts[c], v_ref[sl(c), :], preferred_element_type=F32) for c in chunks]
    states = [st[...]]
    for c in chunks:
        states.append(states[c] * jnp.exp(b_lasts[c]) + kv_ts[c])
    o_inter = [lax.dot_general(q_is[c], states[c].astype(BF16), NT_DIMS, preferred_element_type=F32)
               for c in chunks]
    for c in chunks:
        r = r_ref[sl(c), :]
        o = o_intra[c] + o_inter[c]
        o_ref[sl(c), :] = (_rms(o) * gon * (r * jax.nn.sigmoid(r))).astype(o_ref.dtype)
    state = states[nchunks]
    st[...] = state

    @pl.when(t == pl.num_programs(2) - 1)
    def _():
        sout_ref[...] = state.T


def _gla(qg, kg, vg, gate, rg, g_onorm, s0, batch, seq, rows, chunk):
    has_s0 = s0 is not None
    nchunks = rows // chunk
    r3 = lambda a: a.reshape(batch, seq, a.shape[-1])
    tok = lambda w: pl.BlockSpec((None, rows, w), lambda b, h, t: (b, t, h))
    state_spec = pl.BlockSpec((None, None, GLA_DK, GLA_DV), lambda b, h, t: (b, h, 0, 0))
    in_specs = [tok(GLA_DK), tok(GLA_DK), tok(GLA_DV), tok(GLA_DK), tok(GLA_DV),
                pl.BlockSpec((1, GLA_DV), lambda b, h, t: (0, h))]
    args = [r3(qg), r3(kg), r3(vg), r3(gate), r3(rg), g_onorm]
    if has_s0:
        in_specs.append(state_spec)
        args.append(s0)
    return pl.pallas_call(
        functools.partial(_gla_kernel, chunk=chunk, nchunks=nchunks, has_s0=has_s0),
        grid=(batch, GLA_HEADS, seq // rows),
        in_specs=in_specs,
        out_specs=[tok(GLA_DV), state_spec],
        out_shape=[jax.ShapeDtypeStruct((batch, seq, GLA_WIDTH), BF16),
                   jax.ShapeDtypeStruct((batch, GLA_HEADS, GLA_DK, GLA_DV), F32)],
        scratch_shapes=[pltpu.VMEM((GLA_DV, GLA_DK), F32)],
        compiler_params=pltpu.CompilerParams(
            dimension_semantics=("arbitrary", "arbitrary", "arbitrary"), vmem_limit_bytes=VMEM_LIMIT),
        name="gla",
    )(*args)


def _cumsum_kernel(x_ref, o_ref, *, nblk):
    x = x_ref[...]
    length = nblk * LANES
    r = lax.broadcasted_iota(jnp.int32, (LANES, LANES), 0)
    c = lax.broadcasted_iota(jnp.int32, (LANES, LANES), 1)
    triu = (r <= c).astype(F32)
    li = lax.broadcasted_iota(jnp.int32, (length, LANES), 0)
    ji = lax.broadcasted_iota(jnp.int32, (length, LANES), 1)
    before = (li < ji * LANES).astype(F32)
    offs = jnp.dot(x, before, precision=HIGHEST, preferred_element_type=F32)
    for j in range(nblk):
        blk = x[:, j * LANES:(j + 1) * LANES]
        o_ref[:, j * LANES:(j + 1) * LANES] = (
            jnp.dot(blk, triu, precision=HIGHEST, preferred_element_type=F32) + offs[:, j:j + 1])


def _cumsum(xt, nseg, seglen):
    return pl.pallas_call(
        functools.partial(_cumsum_kernel, nblk=seglen // LANES),
        grid=(nseg,),
        in_specs=[pl.BlockSpec((FOX_HEADS, seglen), lambda s: (0, s))],
        out_specs=pl.BlockSpec((FOX_HEADS, seglen), lambda s: (0, s)),
        out_shape=jax.ShapeDtypeStruct(xt.shape, F32),
        compiler_params=pltpu.CompilerParams(dimension_semantics=("arbitrary",)),
        name="cumsum",
    )(xt)


def _pick_row(block, h):
    sel = lax.broadcasted_iota(jnp.int32, (FOX_HEADS, 1), 0) == h
    return jnp.sum(jnp.where(sel, block, 0.0), axis=0, keepdims=True)


def _pick_col(block, h):
    sel = lax.broadcasted_iota(jnp.int32, (1, FOX_HEADS), 1) == h
    return jnp.sum(jnp.where(sel, block, 0.0), axis=1, keepdims=True)


def _fox_prompt_kernel(q_ref, k_ref, v_ref, crow_ref, ccol_ref, o_ref, s_scr, *, tq, tk, nq):
    h = pl.program_id(1)
    crow = _pick_row(crow_ref[...], h)
    ccol = _pick_col(ccol_ref[...], h)
    rows = lambda qi: slice(qi * tq, (qi + 1) * tq)

    def pieces(qi):
        end = (qi + 1) * tq
        return [(c0, min(c0 + tk, end)) for c0 in range(0, end, tk)]

    def pass1(qi):
        q = q_ref[rows(qi), :]
        cq = ccol[rows(qi), :]
        mx = None
        ps = pieces(qi)
        for n, (c0, c1) in enumerate(ps):
            s = lax.dot_general(q, k_ref[c0:c1, :], NT_DIMS, preferred_element_type=F32)
            s = s + (cq - crow[:, c0:c1])
            if n == len(ps) - 1:
                qpos = lax.broadcasted_iota(jnp.int32, s.shape, 0) + qi * tq
                kpos = lax.broadcasted_iota(jnp.int32, s.shape, 1) + c0
                s = jnp.where(kpos <= qpos, s, NEG)
            s_scr[qi, n, :, :c1 - c0] = s
            part = _lane_fold(s, jnp.maximum)
            mx = part if mx is None else jnp.maximum(mx, part)
        return jnp.max(mx, axis=1, keepdims=True)

    def pass2(qi, m):
        lsum = acc = None
        for n, (c0, c1) in enumerate(pieces(qi)):
            p = jnp.exp(s_scr[qi, n, :, :c1 - c0] - m)
            pv = jnp.dot(p.astype(BF16), v_ref[c0:c1, :], preferred_element_type=F32)
            part = _lane_fold(p, jnp.add)
            lsum = part if lsum is None else lsum + part
            acc = pv if acc is None else acc + pv
        o_ref[rows(qi), :] = (acc / jnp.sum(lsum, axis=1, keepdims=True)).astype(o_ref.dtype)

    m_prev = pass1(0)
    for qi in range(1, nq):
        m_next = pass1(qi)
        pass2(qi - 1, m_prev)
        m_prev = m_next
    pass2(nq - 1, m_prev)


def _fox_prompt(qh, kh, vh, c_row, c_col, batch, seq, tq, tk):
    nq = seq // tq
    whole = pl.BlockSpec((None, None, seq, FOX_HD), lambda b, h: (b, h, 0, 0))
    return pl.pallas_call(
        functools.partial(_fox_prompt_kernel, tq=tq, tk=tk, nq=nq),
        grid=(batch, FOX_HEADS),
        in_specs=[
            whole, whole, whole,
            pl.BlockSpec((FOX_HEADS, seq), lambda b, h: (0, b)),
            pl.BlockSpec((None, seq, FOX_HEADS), lambda b, h: (b, 0, 0)),
        ],
        out_specs=pl.BlockSpec((None, seq, FOX_HD), lambda b, h: (b, 0, h)),
        out_shape=jax.ShapeDtypeStruct((batch, seq, FOX_WIDTH), BF16),
        scratch_shapes=[pltpu.VMEM((nq, seq // tk, tq, tk), F32)],
        compiler_params=pltpu.CompilerParams(
            dimension_semantics=("arbitrary", "arbitrary"), vmem_limit_bytes=VMEM_LIMIT),
        name="fox_prompt",
    )(qh, kh, vh, c_row, c_col)


FOX_TKC = 512


def _fox_sample_kernel(q_ref, kc_ref, vc_ref, kn_ref, vn_ref, cc_ref, cn_ref, cq_ref, o_ref,
                       m_scr, l_scr, acc_scr, *, new):
    c = pl.program_id(1)

    @pl.when(c == 0)
    def _():
        m_scr[...] = jnp.full_like(m_scr, NEG)
        l_scr[...] = jnp.zeros_like(l_scr)
        acc_scr[...] = jnp.zeros_like(acc_scr)

    q = q_ref[...]
    cq = cq_ref[...]

    def logits(k, ck):
        s = jnp.einsum('hqd,hkd->hqk', q, k, preferred_element_type=F32)
        return s + (cq - ck)

    def update(s, v):
        m_old = m_scr[...]
        m_new = jnp.maximum(m_old, jnp.max(s, axis=2, keepdims=True))
        alpha = jnp.exp(m_old - m_new)
        p = jnp.exp(s - m_new)
        l_scr[...] = alpha * l_scr[...] + jnp.sum(p, axis=2, keepdims=True)
        pv = jnp.einsum('hqk,hkd->hqd', p.astype(BF16), v, preferred_element_type=F32)
        acc_scr[...] = alpha * acc_scr[...] + pv
        m_scr[...] = m_new

    def head_rows(ref):
        return jnp.stack([ref[pl.ds(h, FOX_TKC, stride=FOX_HEADS), :]
                          for h in range(FOX_HEADS)]).astype(BF16)

    update(logits(head_rows(kc_ref), cc_ref[...]), head_rows(vc_ref))

    @pl.when(c == pl.num_programs(1) - 1)
    def _():
        row = lax.broadcasted_iota(jnp.int32, (new, new), 0)
        col = lax.broadcasted_iota(jnp.int32, (new, new), 1)
        s = logits(kn_ref[...], cn_ref[:, :, :new])
        update(jnp.where((col <= row)[None], s, NEG), vn_ref[...])
        out = acc_scr[...] / l_scr[...]
        for h in range(FOX_HEADS):
            o_ref[:, h * FOX_HD:(h + 1) * FOX_HD] = out[h].astype(o_ref.dtype)


def _fox_sample(qh, k_hist, v_hist, layer, kh, vh, c_row, c_q, batch, new):
    past = k_hist.shape[2] // FOX_HEADS
    nkc = past // FOX_TKC
    blocks_per_seq = nkc + 1
    heads_new = pl.BlockSpec((None, FOX_HEADS, new, FOX_HD), lambda b, c: (b, 0, 0, 0))
    hist = pl.BlockSpec((None, None, FOX_TKC * FOX_HEADS, FOX_HD), lambda b, c: (layer, b, c, 0))
    return pl.pallas_call(
        functools.partial(_fox_sample_kernel, new=new),
        grid=(batch, nkc),
        in_specs=[heads_new, hist, hist, heads_new, heads_new,
                  pl.BlockSpec((FOX_HEADS, 1, FOX_TKC), lambda b, c: (0, 0, b * blocks_per_seq + c)),
                  pl.BlockSpec((FOX_HEADS, 1, FOX_TKC), lambda b, c: (0, 0, b * blocks_per_seq + nkc)),
                  pl.BlockSpec((None, FOX_HEADS, new, 1), lambda b, c: (b, 0, 0, 0))],
        out_specs=pl.BlockSpec((None, new, FOX_WIDTH), lambda b, c: (b, 0, 0)),
        out_shape=jax.ShapeDtypeStruct((batch, new, FOX_WIDTH), BF16),
        scratch_shapes=[pltpu.VMEM((FOX_HEADS, new, 1), F32),
                        pltpu.VMEM((FOX_HEADS, new, 1), F32),
                        pltpu.VMEM((FOX_HEADS, new, FOX_HD), F32)],
        compiler_params=pltpu.CompilerParams(
            dimension_semantics=("arbitrary", "arbitrary"), vmem_limit_bytes=VMEM_LIMIT),
        name="fox_sample",
    )(qh, k_hist, v_hist, kh, vh, c_row[:, None, :], c_row[:, None, :], c_q)


OUT_PARTS = 2


def _out_proj_kernel(og_ref, of_ref, w_ref, x_ref, gpost_ref, gpre_ref, x1_ref, h2_ref):
    rows = og_ref.shape[0] // OUT_PARTS
    part = lambda a: slice(a * rows, (a + 1) * rows)
    ys = []
    for a in range(OUT_PARTS):
        y = jnp.dot(og_ref[part(a), :], w_ref[:GLA_WIDTH, :], preferred_element_type=F32)
        ys.append(y + jnp.dot(of_ref[part(a), :], w_ref[GLA_WIDTH:, :], preferred_element_type=F32))
    for a in range(OUT_PARTS):
        x1 = x_ref[part(a), :] + _rms(ys[a]) * gpost_ref[...]
        x1_ref[part(a), :] = x1
        h2_ref[part(a), :] = (_rms(x1) * gpre_ref[...]).astype(BF16)


def _out_proj(o_gla, o_fox, w_out, layer, x, g_post, g_mlp_pre, tm):
    m = x.shape[0]
    const = lambda i: (0, 0)
    rowblk = lambda w: pl.BlockSpec((tm, w), lambda i: (i, 0))
    return pl.pallas_call(
        _out_proj_kernel,
        grid=(m // tm,),
        in_specs=[rowblk(GLA_WIDTH), rowblk(FOX_WIDTH),
                  pl.BlockSpec((None, D_MODEL, D_MODEL), lambda i: (layer, 0, 0)),
                  rowblk(D_MODEL),
                  pl.BlockSpec((1, D_MODEL), const), pl.BlockSpec((1, D_MODEL), const)],
        out_specs=[rowblk(D_MODEL), rowblk(D_MODEL)],
        out_shape=[jax.ShapeDtypeStruct((m, D_MODEL), F32), jax.ShapeDtypeStruct((m, D_MODEL), BF16)],
        compiler_params=pltpu.CompilerParams(
            dimension_semantics=("arbitrary",), vmem_limit_bytes=VMEM_LIMIT),
        name="out_proj",
    )(o_gla.reshape(m, GLA_WIDTH), o_fox.reshape(m, FOX_WIDTH), w_out, x, g_post, g_mlp_pre)


MLP_TF = 1024
MLP_TN = 512


def _mlp_kernel(*refs, has_next):
    if has_next:
        h_ref, wu_ref, wd_ref, x1_ref, g_ref, gnext_ref, o_ref, hn_ref, u_scr = refs
    else:
        h_ref, wu_ref, wd_ref, x1_ref, g_ref, o_ref, u_scr = refs
    j = pl.program_id(1)

    def step(first):
        u = jnp.maximum(jnp.dot(h_ref[...], wu_ref[...], preferred_element_type=F32), 0.0)
        u_scr[...] = (u * u).astype(BF16)
        for n in range(0, D_MODEL, MLP_TN):
            part = jnp.dot(u_scr[...], wd_ref[:, n:n + MLP_TN], preferred_element_type=F32)
            if first:
                o_ref[:, n:n + MLP_TN] = part
            else:
                o_ref[:, n:n + MLP_TN] += part

    pl.when(j == 0)(functools.partial(step, True))
    pl.when(j > 0)(functools.partial(step, False))

    @pl.when(j == pl.num_programs(1) - 1)
    def _():
        x2 = x1_ref[...] + _rms(o_ref[...]) * g_ref[...]
        o_ref[...] = x2
        if has_next:
            hn_ref[...] = (_rms(x2) * gnext_ref[...]).astype(BF16)


def _mlp(h2, w_up, w_down, layer, x1, g_post, g_next, tm):
    m = x1.shape[0]
    has_next = g_next is not None
    rowblk = pl.BlockSpec((tm, D_MODEL), lambda i, j: (i, 0))
    gain = pl.BlockSpec((1, D_MODEL), lambda i, j: (0, 0))
    in_specs = [rowblk,
                pl.BlockSpec((None, D_MODEL, MLP_TF), lambda i, j: (layer, 0, j)),
                pl.BlockSpec((None, MLP_TF, D_MODEL), lambda i, j: (layer, j, 0)),
                rowblk, gain]
    args = [h2, w_up, w_down, x1, g_post]
    out_specs = [rowblk]
    out_shape = [jax.ShapeDtypeStruct((m, D_MODEL), F32)]
    if has_next:
        in_specs.append(gain)
        args.append(g_next)
        out_specs.append(rowblk)
        out_shape.append(jax.ShapeDtypeStruct((m, D_MODEL), BF16))
    outs = pl.pallas_call(
        functools.partial(_mlp_kernel, has_next=has_next),
        grid=(m // tm, D_FF // MLP_TF),
        in_specs=in_specs,
        out_specs=out_specs,
        out_shape=out_shape,
        scratch_shapes=[pltpu.VMEM((tm, MLP_TF), BF16)],
        compiler_params=pltpu.CompilerParams(
            dimension_semantics=("arbitrary", "arbitrary"), vmem_limit_bytes=VMEM_LIMIT),
        name="mlp",
    )(*args)
    return (outs[0], outs[1]) if has_next else (outs[0], None)


def _layer_weights(l, g_mix_pre, w_in, w_gla_gate_up, b_gla_gate, b_fox_f, g_gla_onorm,
                   g_mix_post, g_mlp_pre, g_mlp_post):
    row = lambda v: v.reshape(1, -1)
    w_small = jnp.concatenate([w_in[l, :, O_GLR:O_RG], w_in[l, :, O_FL:]], axis=1)
    return dict(
        g_pre=row(g_mix_pre[l]),
        w_a=w_in[l, :, :O_GLR].astype(BF16),
        w_b=w_in[l, :, O_RG:O_FL].astype(BF16),
        w_small=jnp.pad(w_small, ((0, 0), (0, LANES - GLA_RANK - FOX_HEADS))).astype(BF16),
        w_gu=jnp.pad(w_gla_gate_up[l], ((0, LANES - GLA_RANK), (0, 0))).astype(BF16),
        b_gate=row(b_gla_gate[l]),
        b_f=row(b_fox_f[l]),
        b_ft=b_fox_f[l].reshape(-1, 1),
        g_onorm=row(g_gla_onorm[l]),
        g_post=row(g_mix_post[l]),
        g_mlp_pre=row(g_mlp_pre[l]),
        g_mlp_post=row(g_mlp_post[l]),
    )


def _layer(x, h, wts, g_next, batch, seq, layer, kv_prev, tm_proj, tm, gla_rows, gla_chunk,
           s0=None, cache=None):
    qg, kg, vg, rg, gate, logf, logft, qh, kh, vh, kf, vf = _in_proj(
        h, wts, batch, seq, tm_proj, layer, kv_prev)
    o_gla, s_new = _gla(qg, kg, vg, gate, rg, wts["g_onorm"], s0, batch, seq, gla_rows, gla_chunk)
    if cache is None:
        c_row = _cumsum(logft, batch, seq)
        c_col = c_row.T.reshape(batch, seq, FOX_HEADS)
        o_fox = _fox_prompt(qh, kh, vh, c_row, c_col, batch, seq, tq=256, tk=512)
    else:
        k_hist, v_hist, logf_cache = cache
        past = logf_cache.shape[1]
        total = past + seq
        seg = past + FOX_TKC
        lf = jnp.concatenate([logf_cache, logf.reshape(batch, seq, FOX_HEADS)], axis=1)
        lf = jnp.pad(lf.transpose(2, 0, 1), ((0, 0), (0, 0), (0, seg - total)))
        c_row = _cumsum(lf.reshape(FOX_HEADS, batch * seg), batch, seg)
        c_q = c_row.reshape(FOX_HEADS, batch, seg)[:, :, past:total].transpose(1, 0, 2)[..., None]
        o_fox = _fox_sample(qh, k_hist, v_hist, layer, kh, vh, c_row, c_q, batch, seq)
    x1, h2 = _out_proj(o_gla, o_fox, wts["w_out"], layer, x, wts["g_post"], wts["g_mlp_pre"], tm)
    x2, h_next = _mlp(h2, wts["w_up"], wts["w_down"], layer, x1, wts["g_mlp_post"], g_next, tm)
    return x2, h_next, (kf, vf), logf.reshape(batch, seq, FOX_HEADS), s_new


def kernel(x_prompt, x_sample, cache_fox_k, cache_fox_v, cache_fox_logf, state_gla, g_mix_pre, w_in,
           w_gla_gate_up, b_gla_gate, b_fox_f, g_gla_onorm, w_out, g_mix_post, g_mlp_pre, w_mlp_up,
           w_mlp_down, g_mlp_post):
    pb, pl_, _ = x_prompt.shape
    sb, sl, _ = x_sample.shape
    yp = x_prompt.reshape(pb * pl_, D_MODEL)
    ys = x_sample.reshape(sb * sl, D_MODEL)
    past = cache_fox_k.shape[2]
    k_hist = cache_fox_k.reshape(DEPTH, sb, past * FOX_HEADS, FOX_HD)
    v_hist = cache_fox_v.reshape(DEPTH, sb, past * FOX_HEADS, FOX_HD)
    g0 = g_mix_pre[0].reshape(1, -1)
    hp = _norm(yp, g0, 512)
    hs = _norm(ys, g0, sb * sl)
    kv_p = kv_s = None
    logf_p, logf_s, state_p, state_s = [], [], [], []
    big = dict(w_out=w_out.astype(BF16), w_up=w_mlp_up.astype(BF16), w_down=w_mlp_down.astype(BF16))
    for l in range(DEPTH):
        wts = _layer_weights(l, g_mix_pre, w_in, w_gla_gate_up, b_gla_gate, b_fox_f, g_gla_onorm,
                             g_mix_post, g_mlp_pre, g_mlp_post)
        wts.update(big)
        g_next = g_mix_pre[l + 1].reshape(1, -1) if l + 1 < DEPTH else None
        yp, hp, kv_p, lf, st = _layer(yp, hp, wts, g_next, pb, pl_, l, kv_p, tm_proj=256, tm=512,
                                      gla_rows=1024, gla_chunk=CHUNK)
        logf_p.append(lf)
        state_p.append(st)
        ys, hs, kv_s, lf, st = _layer(ys, hs, wts, g_next, sb, sl, l, kv_s, tm_proj=sb * sl,
                                      tm=sb * sl, gla_rows=sl, gla_chunk=sl, s0=state_gla[l],
                                      cache=(k_hist, v_hist, cache_fox_logf[l]))
        logf_s.append(lf)
        state_s.append(st)
    kv5 = lambda a, b, s: a.reshape(DEPTH, b, s, FOX_HEADS, FOX_HD)
    return (yp.reshape(x_prompt.shape), ys.reshape(x_sample.shape),
            kv5(kv_p[0], pb, pl_), kv5(kv_p[1], pb, pl_), jnp.stack(logf_p), jnp.stack(state_p),
            kv5(kv_s[0], sb, sl), kv5(kv_s[1], sb, sl), jnp.stack(logf_s), jnp.stack(state_s))
```

```python
import functools

import jax
import jax.numpy as jnp
from jax import lax
from jax.experimental import pallas as pl
from jax.experimental.pallas import tpu as pltpu

F32 = jnp.float32
BF16 = jnp.bfloat16
HIGHEST = lax.Precision.HIGHEST

D_MODEL = 2048
DEPTH = 2
CHUNK = 64
GLA_HEADS = 4
GLA_DK = 128
GLA_DV = 256
GLA_QK = GLA_HEADS * GLA_DK
GLA_WIDTH = GLA_HEADS * GLA_DV
GLA_RANK = 16
GLA_TAU = 16.0
FOX_HEADS = 8
FOX_HD = 128
FOX_WIDTH = FOX_HEADS * FOX_HD
D_FF = 4 * D_MODEL
EPS = 1e-6
NEG = -1e30

O_GLR = 2 * GLA_QK + GLA_WIDTH
O_RG = O_GLR + GLA_RANK
O_FL = O_RG + GLA_WIDTH + 3 * FOX_WIDTH
PROJ_WIDTH = O_FL + FOX_HEADS
MAIN_WIDTH = PROJ_WIDTH - GLA_RANK - FOX_HEADS
LANES = 128
SMALL_FL = GLA_RANK

VMEM_LIMIT = 52 * 1024 * 1024
NT_DIMS = (((1,), (1,)), ((), ()))
TN_DIMS = (((0,), (0,)), ((), ()))


def _log_sigmoid(x):
    return jnp.minimum(x, 0.0) - jnp.log1p(jnp.exp(-jnp.abs(x)))


def _rms(x):
    return x * lax.rsqrt(jnp.mean(x * x, axis=-1, keepdims=True) + EPS)


def _lane_fold(x, op):
    parts = [x[:, i * LANES:(i + 1) * LANES] for i in range(x.shape[1] // LANES)]
    while len(parts) > 1:
        parts = [op(parts[i], parts[i + 1]) for i in range(0, len(parts), 2)]
    return parts[0]


def _norm_kernel(x_ref, g_ref, h_ref):
    h_ref[...] = (_rms(x_ref[...]) * g_ref[...]).astype(BF16)


def _norm(x, g, tm):
    m = x.shape[0]
    return pl.pallas_call(
        _norm_kernel,
        grid=(m // tm,),
        in_specs=[pl.BlockSpec((tm, D_MODEL), lambda i: (i, 0)),
                  pl.BlockSpec((1, D_MODEL), lambda i: (0, 0))],
        out_specs=pl.BlockSpec((tm, D_MODEL), lambda i: (i, 0)),
        out_shape=jax.ShapeDtypeStruct((m, D_MODEL), BF16),
        compiler_params=pltpu.CompilerParams(
            dimension_semantics=("arbitrary",), vmem_limit_bytes=VMEM_LIMIT),
        name="norm",
    )(x, g)


CAST_ROWS = 512


def _cast_kernel(w_ref, o_ref):
    o_ref[...] = w_ref[...].astype(BF16)


def _cast_w_in(w_t, layer):
    return pl.pallas_call(
        _cast_kernel,
        grid=(pl.cdiv(PROJ_WIDTH, CAST_ROWS),),
        in_specs=[pl.BlockSpec((None, CAST_ROWS, D_MODEL), lambda i: (layer, i, 0))],
        out_specs=pl.BlockSpec((CAST_ROWS, D_MODEL), lambda i: (i, 0)),
        out_shape=jax.ShapeDtypeStruct((PROJ_WIDTH, D_MODEL), BF16),
        compiler_params=pltpu.CompilerParams(
            dimension_semantics=("arbitrary",), vmem_limit_bytes=VMEM_LIMIT),
        name="cast_w_in",
    )(w_t)


IN_TN = 512
HEADS_PER_TN = IN_TN // FOX_HD
GLA_BLOCKS = O_GLR // IN_TN
N_IN = 7
N_OUT = 12
IN_PROJ_VMEM_LIMIT = 58 * 1024 * 1024


def _in_proj_kernel(*refs, nb, lt, n_alias, n_cast):
    h_ref, wt_ref, ws_ref, wgu_ref, bg_ref, bf_ref, bft_ref = refs[:N_IN]
    cast_in = refs[N_IN:N_IN + n_cast]
    outs = refs[N_IN + n_cast + n_alias:]
    (qg_ref, kg_ref, vg_ref, rg_ref, gate_ref, logf_ref, logft_ref,
     qh_ref, kh_ref, vh_ref, kf_ref, vf_ref) = outs[:N_OUT]
    tm = nb * lt
    h = h_ref[...]

    for src, dst in zip(cast_in, outs[N_OUT:]):
        dst[...] = src[...].astype(BF16)

    zs = lax.dot_general(h, ws_ref[...], NT_DIMS, preferred_element_type=F32)
    zst = lax.dot_general(ws_ref[...], h, NT_DIMS, preferred_element_type=F32)
    gpre = jnp.dot(zs.astype(BF16), wgu_ref[...], preferred_element_type=F32) + bg_ref[...]
    gate_ref[...] = _log_sigmoid(gpre) / GLA_TAU
    logf_ref[...] = _log_sigmoid(zs[:, SMALL_FL:SMALL_FL + FOX_HEADS] + bf_ref[...])
    logft_ref[...] = _log_sigmoid(zst[SMALL_FL:SMALL_FL + FOX_HEADS, :] + bft_ref[...])

    def mm(cb):
        r0 = cb * IN_TN + (GLA_RANK if cb >= GLA_BLOCKS else 0)
        return lax.dot_general(h, wt_ref[r0:r0 + IN_TN, :], NT_DIMS, preferred_element_type=F32)

    def heads(ref, val, half):
        for hh in range(HEADS_PER_TN):
            ref[:, half * HEADS_PER_TN + hh] = (
                val[:, hh * FOX_HD:(hh + 1) * FOX_HD].astype(BF16).reshape(nb, lt, FOX_HD))

    def rows(ref, val, half):
        for hh in range(HEADS_PER_TN):
            ref[pl.ds(half * HEADS_PER_TN + hh, tm, stride=FOX_HEADS), :] = (
                val[:, hh * FOX_HD:(hh + 1) * FOX_HD])

    qg_ref[...] = mm(0)
    kg_ref[...] = mm(1)
    for half in range(2):
        cols = slice(half * IN_TN, (half + 1) * IN_TN)
        vg_ref[:, cols] = mm(2 + half).astype(BF16)
        rg_ref[:, cols] = mm(4 + half)
        heads(qh_ref, mm(6 + half) * (FOX_HD ** -0.5), half)
        val = mm(8 + half)
        heads(kh_ref, val, half)
        rows(kf_ref, val, half)
        val = mm(10 + half)
        heads(vh_ref, val, half)
        rows(vf_ref, val, half)


def _in_proj(h, wts, batch, seq, tm, layer, kv_prev, cast=()):
    m = h.shape[0]
    steps = m // tm
    lt = min(tm, seq)
    nb = tm // lt
    nt = seq // lt
    const = lambda i: (0, 0)
    rowblk = lambda w: pl.BlockSpec((tm, w), lambda i: (i, 0))
    head_shape = jax.ShapeDtypeStruct((batch, FOX_HEADS, seq, FOX_HD), BF16)
    head_spec = pl.BlockSpec((nb, FOX_HEADS, lt, FOX_HD), lambda i: (i // nt, 0, i % nt, 0))
    kv_shape = jax.ShapeDtypeStruct((DEPTH, m * FOX_HEADS, FOX_HD), F32)
    kv_spec = pl.BlockSpec((None, tm * FOX_HEADS, FOX_HD), lambda i: (layer, i, 0))
    out_shape = [
        jax.ShapeDtypeStruct((m, GLA_QK), F32), jax.ShapeDtypeStruct((m, GLA_QK), F32),
        jax.ShapeDtypeStruct((m, GLA_WIDTH), BF16), jax.ShapeDtypeStruct((m, GLA_WIDTH), F32),
        jax.ShapeDtypeStruct((m, GLA_QK), F32), jax.ShapeDtypeStruct((m, FOX_HEADS), F32),
        jax.ShapeDtypeStruct((FOX_HEADS, m), F32),
        head_shape, head_shape, head_shape, kv_shape, kv_shape,
    ]
    out_specs = [
        rowblk(GLA_QK), rowblk(GLA_QK), rowblk(GLA_WIDTH), rowblk(GLA_WIDTH),
        rowblk(GLA_QK), rowblk(FOX_HEADS),
        pl.BlockSpec((FOX_HEADS, tm), lambda i: (0, i)),
        head_spec, head_spec, head_spec, kv_spec, kv_spec,
    ]
    assert len(out_specs) == N_OUT
    in_specs = [
        rowblk(D_MODEL),
        pl.BlockSpec((PROJ_WIDTH, D_MODEL), const),
        pl.BlockSpec((LANES, D_MODEL), const),
        pl.BlockSpec((LANES, GLA_QK), const),
        pl.BlockSpec((1, GLA_QK), const),
        pl.BlockSpec((1, FOX_HEADS), const),
        pl.BlockSpec((FOX_HEADS, 1), const),
    ]
    args = [h, wts["w_t"], wts["w_small_t"], wts["w_gu"], wts["b_gate"], wts["b_f"], wts["b_ft"]]
    assert len(args) == N_IN
    for w in cast:
        _, r, c = w.shape
        in_specs.append(pl.BlockSpec((None, r // steps, c), lambda i: (layer, i, 0)))
        out_specs.append(pl.BlockSpec((r // steps, c), lambda i: (i, 0)))
        out_shape.append(jax.ShapeDtypeStruct((r, c), BF16))
        args.append(w)
    aliases = {}
    if kv_prev is not None:
        in_specs += [pl.BlockSpec(memory_space=pl.ANY)] * 2
        aliases = {len(args): N_OUT - 2, len(args) + 1: N_OUT - 1}
        args += list(kv_prev)
    return pl.pallas_call(
        functools.partial(_in_proj_kernel, nb=nb, lt=lt, n_alias=len(aliases), n_cast=len(cast)),
        grid=(steps,),
        in_specs=in_specs,
        out_specs=out_specs,
        out_shape=out_shape,
        input_output_aliases=aliases,
        compiler_params=pltpu.CompilerParams(
            dimension_semantics=("arbitrary",), vmem_limit_bytes=IN_PROJ_VMEM_LIMIT),
        name="in_proj",
    )(*args)


def _gla_kernel(*refs, chunk, nchunks, has_s0):
    if has_s0:
        q_ref, k_ref, v_ref, g_ref, r_ref, gon_ref, s0_ref, o_ref, sout_ref, st = refs
    else:
        q_ref, k_ref, v_ref, g_ref, r_ref, gon_ref, o_ref, sout_ref, st = refs
    t = pl.program_id(2)

    @pl.when(t == 0)
    def _():
        if has_s0:
            st[...] = s0_ref[...].T
        else:
            st[...] = jnp.zeros_like(st)

    row = lax.broadcasted_iota(jnp.int32, (chunk, chunk), 0)
    col = lax.broadcasted_iota(jnp.int32, (chunk, chunk), 1)
    causal = col <= row
    tri = causal.astype(F32)
    qscale = GLA_DK ** -0.5
    gon = gon_ref[...]
    chunks = range(nchunks)
    sl = lambda c: pl.ds(c * chunk, chunk)
    bs = [jnp.dot(tri, g_ref[sl(c), :], precision=HIGHEST, preferred_element_type=F32)
          for c in chunks]
    b_lasts = [b[chunk - 1:chunk, :] for b in bs]
    q_is = [(q_ref[sl(c), :] * qscale * jnp.exp(bs[c])).astype(BF16) for c in chunks]
    k_is = [(k_ref[sl(c), :] * jnp.exp(-bs[c])).astype(BF16) for c in chunks]
    k_es = [(k_ref[sl(c), :] * jnp.exp(b_lasts[c] - bs[c])).astype(BF16) for c in chunks]
    atts = [lax.dot_general(q_is[c], k_is[c], NT_DIMS, preferred_element_type=F32) for c in chunks]
    kv_ts = [lax.dot_general(v_ref[sl(c), :], k_es[c], TN_DIMS, preferred_element_type=F32)
             for c in chunks]
    atts = [jnp.where(causal, a, 0.0).astype(BF16) for a in atts]
    o_intra = [jnp.dot(atts[c], v_ref[sl(c), :], preferred_element_type=F32) for c in chunks]
    states = [st[...]]
    for c in chunks:
        states.append(states[c] * jnp.exp(b_lasts[c]) + kv_ts[c])
    o_inter = [lax.dot_general(q_is[c], states[c].astype(BF16), NT_DIMS, preferred_element_type=F32)
               for c in chunks]
    for c in chunks:
        r = r_ref[sl(c), :]
        o = o_intra[c] + o_inter[c]
        o_ref[sl(c), :] = (_rms(o) * gon * (r * jax.nn.sigmoid(r))).astype(o_ref.dtype)
    state = states[nchunks]
    st[...] = state

    @pl.when(t == pl.num_programs(2) - 1)
    def _():
        sout_ref[...] = state.T


def _gla(qg, kg, vg, gate, rg, g_onorm, s0, batch, seq, rows, chunk):
    has_s0 = s0 is not None
    nchunks = rows // chunk
    r3 = lambda a: a.reshape(batch, seq, a.shape[-1])
    tok = lambda w: pl.BlockSpec((None, rows, w), lambda b, h, t: (b, t, h))
    state_spec = pl.BlockSpec((None, None, GLA_DK, GLA_DV), lambda b, h, t: (b, h, 0, 0))
    in_specs = [tok(GLA_DK), tok(GLA_DK), tok(GLA_DV), tok(GLA_DK), tok(GLA_DV),
                pl.BlockSpec((1, GLA_DV), lambda b, h, t: (0, h))]
    args = [r3(qg), r3(kg), r3(vg), r3(gate), r3(rg), g_onorm]
    if has_s0:
        in_specs.append(state_spec)
        args.append(s0)
    return pl.pallas_call(
        functools.partial(_gla_kernel, chunk=chunk, nchunks=nchunks, has_s0=has_s0),
        grid=(batch, GLA_HEADS, seq // rows),
        in_specs=in_specs,
        out_specs=[tok(GLA_DV), state_spec],
        out_shape=[jax.ShapeDtypeStruct((batch, seq, GLA_WIDTH), BF16),
                   jax.ShapeDtypeStruct((batch, GLA_HEADS, GLA_DK, GLA_DV), F32)],
        scratch_shapes=[pltpu.VMEM((GLA_DV, GLA_DK), F32)],
        compiler_params=pltpu.CompilerParams(
            dimension_semantics=("arbitrary", "arbitrary", "arbitrary"), vmem_limit_bytes=VMEM_LIMIT),
        name="gla",
    )(*args)


def _cumsum_kernel(x_ref, o_ref, *, nblk):
    x = x_ref[...]
    length = nblk * LANES
    r = lax.broadcasted_iota(jnp.int32, (LANES, LANES), 0)
    c = lax.broadcasted_iota(jnp.int32, (LANES, LANES), 1)
    triu = (r <= c).astype(F32)
    li = lax.broadcasted_iota(jnp.int32, (length, LANES), 0)
    ji = lax.broadcasted_iota(jnp.int32, (length, LANES), 1)
    before = (li < ji * LANES).astype(F32)
    offs = jnp.dot(x, before, precision=HIGHEST, preferred_element_type=F32)
    for j in range(nblk):
        blk = x[:, j * LANES:(j + 1) * LANES]
        o_ref[:, j * LANES:(j + 1) * LANES] = (
            jnp.dot(blk, triu, precision=HIGHEST, preferred_element_type=F32) + offs[:, j:j + 1])


def _cumsum(xt):
    spec = pl.BlockSpec(xt.shape, lambda i: (0, 0))
    return pl.pallas_call(
        functools.partial(_cumsum_kernel, nblk=xt.shape[1] // LANES),
        grid=(1,),
        in_specs=[spec],
        out_specs=spec,
        out_shape=jax.ShapeDtypeStruct(xt.shape, F32),
        compiler_params=pltpu.CompilerParams(dimension_semantics=("arbitrary",)),
        name="cumsum",
    )(xt)


def _pick_row(block, h):
    sel = lax.broadcasted_iota(jnp.int32, (FOX_HEADS, 1), 0) == h
    return jnp.sum(jnp.where(sel, block, 0.0), axis=0, keepdims=True)


def _pick_col(block, h):
    sel = lax.broadcasted_iota(jnp.int32, (1, FOX_HEADS), 1) == h
    return jnp.sum(jnp.where(sel, block, 0.0), axis=1, keepdims=True)


def _fox_prompt_kernel(q_ref, k_ref, v_ref, crow_ref, ccol_ref, o_ref, s_scr, *, tq, tk, nq):
    h = pl.program_id(1)
    crow = _pick_row(crow_ref[...], h)
    ccol = _pick_col(ccol_ref[...], h)
    rows = lambda qi: slice(qi * tq, (qi + 1) * tq)

    def pieces(qi):
        end = (qi + 1) * tq
        return [(c0, min(c0 + tk, end)) for c0 in range(0, end, tk)]

    def pass1(qi):
        q = q_ref[rows(qi), :]
        cq = ccol[rows(qi), :]
        mx = None
        ps = pieces(qi)
        for n, (c0, c1) in enumerate(ps):
            s = lax.dot_general(q, k_ref[c0:c1, :], NT_DIMS, preferred_element_type=F32)
            s = s + (cq - crow[:, c0:c1])
            if n == len(ps) - 1:
                qpos = lax.broadcasted_iota(jnp.int32, s.shape, 0) + qi * tq
                kpos = lax.broadcasted_iota(jnp.int32, s.shape, 1) + c0
                s = jnp.where(kpos <= qpos, s, NEG)
            s_scr[qi, n, :, :c1 - c0] = s
            part = _lane_fold(s, jnp.maximum)
            mx = part if mx is None else jnp.maximum(mx, part)
        return jnp.max(mx, axis=1, keepdims=True)

    def pass2(qi, m):
        lsum = acc = None
        for n, (c0, c1) in enumerate(pieces(qi)):
            p = jnp.exp(s_scr[qi, n, :, :c1 - c0] - m)
            pv = jnp.dot(p.astype(BF16), v_ref[c0:c1, :], preferred_element_type=F32)
            part = _lane_fold(p, jnp.add)
            lsum = part if lsum is None else lsum + part
            acc = pv if acc is None else acc + pv
        o_ref[rows(qi), :] = (acc / jnp.sum(lsum, axis=1, keepdims=True)).astype(o_ref.dtype)

    m_prev = pass1(0)
    for qi in range(1, nq):
        m_next = pass1(qi)
        pass2(qi - 1, m_prev)
        m_prev = m_next
    pass2(nq - 1, m_prev)


def _fox_prompt(qh, kh, vh, c_row, c_col, batch, seq, tq, tk):
    nq = seq // tq
    whole = pl.BlockSpec((None, None, seq, FOX_HD), lambda b, h: (b, h, 0, 0))
    return pl.pallas_call(
        functools.partial(_fox_prompt_kernel, tq=tq, tk=tk, nq=nq),
        grid=(batch, FOX_HEADS),
        in_specs=[
            whole, whole, whole,
            pl.BlockSpec((FOX_HEADS, seq), lambda b, h: (b, 0)),
            pl.BlockSpec((None, seq, FOX_HEADS), lambda b, h: (b, 0, 0)),
        ],
        out_specs=pl.BlockSpec((None, seq, FOX_HD), lambda b, h: (b, 0, h)),
        out_shape=jax.ShapeDtypeStruct((batch, seq, FOX_WIDTH), BF16),
        scratch_shapes=[pltpu.VMEM((nq, seq // tk, tq, tk), F32)],
        compiler_params=pltpu.CompilerParams(
            dimension_semantics=("arbitrary", "arbitrary"), vmem_limit_bytes=VMEM_LIMIT),
        name="fox_prompt",
    )(qh, kh, vh, c_row, c_col)


FOX_TKC = 512


def _fox_sample_kernel(q_ref, kc_ref, vc_ref, kn_ref, vn_ref, cc_ref, cn_ref, cq_ref, o_ref,
                       m_scr, l_scr, acc_scr, *, new):
    c = pl.program_id(1)

    @pl.when(c == 0)
    def _():
        m_scr[...] = jnp.full_like(m_scr, NEG)
        l_scr[...] = jnp.zeros_like(l_scr)
        acc_scr[...] = jnp.zeros_like(acc_scr)

    q = q_ref[...]
    cq = cq_ref[...]

    def logits(k, ck):
        s = jnp.einsum('hqd,hkd->hqk', q, k, preferred_element_type=F32)
        return s + (cq - ck)

    def update(s, v):
        m_old = m_scr[...]
        m_new = jnp.maximum(m_old, jnp.max(s, axis=2, keepdims=True))
        alpha = jnp.exp(m_old - m_new)
        p = jnp.exp(s - m_new)
        l_scr[...] = alpha * l_scr[...] + jnp.sum(p, axis=2, keepdims=True)
        pv = jnp.einsum('hqk,hkd->hqd', p.astype(BF16), v, preferred_element_type=F32)
        acc_scr[...] = alpha * acc_scr[...] + pv
        m_scr[...] = m_new

    def head_rows(ref):
        return jnp.stack([ref[pl.ds(h, FOX_TKC, stride=FOX_HEADS), :]
                          for h in range(FOX_HEADS)]).astype(BF16)

    update(logits(head_rows(kc_ref), cc_ref[...]), head_rows(vc_ref))

    @pl.when(c == pl.num_programs(1) - 1)
    def _():
        row = lax.broadcasted_iota(jnp.int32, (new, new), 0)
        col = lax.broadcasted_iota(jnp.int32, (new, new), 1)
        s = logits(kn_ref[...], cn_ref[:, :, :new])
        update(jnp.where((col <= row)[None], s, NEG), vn_ref[...])
        out = acc_scr[...] / l_scr[...]
        for h in range(FOX_HEADS):
            o_ref[:, h * FOX_HD:(h + 1) * FOX_HD] = out[h].astype(o_ref.dtype)


def _fox_sample(qh, k_hist, v_hist, layer, kh, vh, c_row, c_q, batch, new):
    past = k_hist.shape[2] // FOX_HEADS
    nkc = past // FOX_TKC
    heads_new = pl.BlockSpec((None, FOX_HEADS, new, FOX_HD), lambda b, c: (b, 0, 0, 0))
    hist = pl.BlockSpec((None, None, FOX_TKC * FOX_HEADS, FOX_HD), lambda b, c: (layer, b, c, 0))
    return pl.pallas_call(
        functools.partial(_fox_sample_kernel, new=new),
        grid=(batch, nkc),
        in_specs=[heads_new, hist, hist, heads_new, heads_new,
                  pl.BlockSpec((FOX_HEADS, 1, FOX_TKC), lambda b, c: (b, 0, c)),
                  pl.BlockSpec((FOX_HEADS, 1, FOX_TKC), lambda b, c: (b, 0, nkc)),
                  pl.BlockSpec((None, FOX_HEADS, new, 1), lambda b, c: (b, 0, 0, 0))],
        out_specs=pl.BlockSpec((None, new, FOX_WIDTH), lambda b, c: (b, 0, 0)),
        out_shape=jax.ShapeDtypeStruct((batch, new, FOX_WIDTH), BF16),
        scratch_shapes=[pltpu.VMEM((FOX_HEADS, new, 1), F32),
                        pltpu.VMEM((FOX_HEADS, new, 1), F32),
                        pltpu.VMEM((FOX_HEADS, new, FOX_HD), F32)],
        compiler_params=pltpu.CompilerParams(
            dimension_semantics=("arbitrary", "arbitrary"), vmem_limit_bytes=VMEM_LIMIT),
        name="fox_sample",
    )(qh, k_hist, v_hist, kh, vh, c_row[:, None, :], c_row[:, None, :], c_q)


OUT_PARTS = 2


def _out_proj_kernel(og_ref, of_ref, w_ref, x_ref, gpost_ref, gpre_ref, x1_ref, h2_ref):
    rows = og_ref.shape[0] // OUT_PARTS
    part = lambda a: slice(a * rows, (a + 1) * rows)
    ys = []
    for a in range(OUT_PARTS):
        y = jnp.dot(og_ref[part(a), :], w_ref[:GLA_WIDTH, :], preferred_element_type=F32)
        ys.append(y + jnp.dot(of_ref[part(a), :], w_ref[GLA_WIDTH:, :], preferred_element_type=F32))
    for a in range(OUT_PARTS):
        x1 = x_ref[part(a), :] + _rms(ys[a]) * gpost_ref[...]
        x1_ref[part(a), :] = x1
        h2_ref[part(a), :] = (_rms(x1) * gpre_ref[...]).astype(BF16)


def _out_proj(o_gla, o_fox, w_out, x, g_post, g_mlp_pre, tm):
    m = x.shape[0]
    const = lambda i: (0, 0)
    rowblk = lambda w: pl.BlockSpec((tm, w), lambda i: (i, 0))
    return pl.pallas_call(
        _out_proj_kernel,
        grid=(m // tm,),
        in_specs=[rowblk(GLA_WIDTH), rowblk(FOX_WIDTH),
                  pl.BlockSpec((D_MODEL, D_MODEL), const),
                  rowblk(D_MODEL),
                  pl.BlockSpec((1, D_MODEL), const), pl.BlockSpec((1, D_MODEL), const)],
        out_specs=[rowblk(D_MODEL), rowblk(D_MODEL)],
        out_shape=[jax.ShapeDtypeStruct((m, D_MODEL), F32), jax.ShapeDtypeStruct((m, D_MODEL), BF16)],
        compiler_params=pltpu.CompilerParams(
            dimension_semantics=("arbitrary",), vmem_limit_bytes=VMEM_LIMIT),
        name="out_proj",
    )(o_gla.reshape(m, GLA_WIDTH), o_fox.reshape(m, FOX_WIDTH), w_out, x, g_post, g_mlp_pre)


MLP_TF = 1024
MLP_TN = 512


def _mlp_kernel(*refs, has_next):
    if has_next:
        h_ref, wu_ref, wd_ref, x1_ref, g_ref, gnext_ref, o_ref, hn_ref, u_scr = refs
    else:
        h_ref, wu_ref, wd_ref, x1_ref, g_ref, o_ref, u_scr = refs
    j = pl.program_id(1)

    def step(first):
        u = jnp.maximum(jnp.dot(h_ref[...], wu_ref[...], preferred_element_type=F32), 0.0)
        u_scr[...] = (u * u).astype(BF16)
        for n in range(0, D_MODEL, MLP_TN):
            part = jnp.dot(u_scr[...], wd_ref[:, n:n + MLP_TN], preferred_element_type=F32)
            if first:
                o_ref[:, n:n + MLP_TN] = part
            else:
                o_ref[:, n:n + MLP_TN] += part

    pl.when(j == 0)(functools.partial(step, True))
    pl.when(j > 0)(functools.partial(step, False))

    @pl.when(j == pl.num_programs(1) - 1)
    def _():
        x2 = x1_ref[...] + _rms(o_ref[...]) * g_ref[...]
        o_ref[...] = x2
        if has_next:
            hn_ref[...] = (_rms(x2) * gnext_ref[...]).astype(BF16)


def _mlp(h2, w_up, w_down, x1, g_post, g_next, tm):
    m = x1.shape[0]
    has_next = g_next is not None
    rowblk = pl.BlockSpec((tm, D_MODEL), lambda i, j: (i, 0))
    gain = pl.BlockSpec((1, D_MODEL), lambda i, j: (0, 0))
    in_specs = [rowblk,
                pl.BlockSpec((D_MODEL, MLP_TF), lambda i, j: (0, j)),
                pl.BlockSpec((MLP_TF, D_MODEL), lambda i, j: (j, 0)),
                rowblk, gain]
    args = [h2, w_up, w_down, x1, g_post]
    out_specs = [rowblk]
    out_shape = [jax.ShapeDtypeStruct((m, D_MODEL), F32)]
    if has_next:
        in_specs.append(gain)
        args.append(g_next)
        out_specs.append(rowblk)
        out_shape.append(jax.ShapeDtypeStruct((m, D_MODEL), BF16))
    outs = pl.pallas_call(
        functools.partial(_mlp_kernel, has_next=has_next),
        grid=(m // tm, D_FF // MLP_TF),
        in_specs=in_specs,
        out_specs=out_specs,
        out_shape=out_shape,
        scratch_shapes=[pltpu.VMEM((tm, MLP_TF), BF16)],
        compiler_params=pltpu.CompilerParams(
            dimension_semantics=("arbitrary", "arbitrary"), vmem_limit_bytes=VMEM_LIMIT),
        name="mlp",
    )(*args)
    return (outs[0], outs[1]) if has_next else (outs[0], None)


def _layer_weights(l, g_mix_pre, w_in, w_gla_gate_up, b_gla_gate, b_fox_f, g_gla_onorm,
                   g_mix_post, g_mlp_pre, g_mlp_post):
    row = lambda v: v.reshape(1, -1)
    w_t = _cast_w_in(jnp.swapaxes(w_in, 1, 2), l)
    w_small_t = jnp.concatenate([w_t[O_GLR:O_RG], w_t[O_FL:]], axis=0)
    return dict(
        g_pre=row(g_mix_pre[l]),
        w_t=w_t,
        w_small_t=jnp.pad(w_small_t, ((0, LANES - GLA_RANK - FOX_HEADS), (0, 0))),
        w_gu=jnp.pad(w_gla_gate_up[l], ((0, LANES - GLA_RANK), (0, 0))).astype(BF16),
        b_gate=row(b_gla_gate[l]),
        b_f=row(b_fox_f[l]),
        b_ft=b_fox_f[l].reshape(-1, 1),
        g_onorm=row(g_gla_onorm[l]),
        g_post=row(g_mix_post[l]),
        g_mlp_pre=row(g_mlp_pre[l]),
        g_mlp_post=row(g_mlp_post[l]),
    )


def _layer(x, h, wts, g_next, batch, seq, layer, kv_prev, tm_proj, tm, gla_rows, gla_chunk,
           s0=None, cache=None, cast=None):
    outs = _in_proj(h, wts, batch, seq, tm_proj, layer, kv_prev, cast or ())
    qg, kg, vg, rg, gate, logf, logft, qh, kh, vh, kf, vf = outs[:N_OUT]
    if cast:
        wts["w_up"], wts["w_down"], wts["w_out"] = outs[N_OUT:]
    o_gla, s_new = _gla(qg, kg, vg, gate, rg, wts["g_onorm"], s0, batch, seq, gla_rows, gla_chunk)
    if cache is None:
        lf = logft.reshape(FOX_HEADS, batch, seq).transpose(1, 0, 2)
        c_row = _cumsum(lf.reshape(batch * FOX_HEADS, seq))
        c_col = c_row.reshape(batch, FOX_HEADS, seq).transpose(0, 2, 1)
        o_fox = _fox_prompt(qh, kh, vh, c_row, c_col, batch, seq, tq=256, tk=512)
    else:
        k_hist, v_hist, logf_cache = cache
        past = logf_cache.shape[1]
        total = past + seq
        seg = past + FOX_TKC
        lf = jnp.concatenate([logf_cache, logf.reshape(batch, seq, FOX_HEADS)], axis=1)
        lf = jnp.pad(lf.transpose(0, 2, 1), ((0, 0), (0, 0), (0, seg - total)))
        c_row = _cumsum(lf.reshape(batch * FOX_HEADS, seg))
        c_q = c_row.reshape(batch, FOX_HEADS, seg)[:, :, past:total][..., None]
        o_fox = _fox_sample(qh, k_hist, v_hist, layer, kh, vh, c_row, c_q, batch, seq)
    x1, h2 = _out_proj(o_gla, o_fox, wts["w_out"], x, wts["g_post"], wts["g_mlp_pre"], tm)
    x2, h_next = _mlp(h2, wts["w_up"], wts["w_down"], x1, wts["g_mlp_post"], g_next, tm)
    return x2, h_next, (kf, vf), logf.reshape(batch, seq, FOX_HEADS), s_new


def kernel(x_prompt, x_sample, cache_fox_k, cache_fox_v, cache_fox_logf, state_gla, g_mix_pre, w_in,
           w_gla_gate_up, b_gla_gate, b_fox_f, g_gla_onorm, w_out, g_mix_post, g_mlp_pre, w_mlp_up,
           w_mlp_down, g_mlp_post):
    pb, pl_, _ = x_prompt.shape
    sb, sl, _ = x_sample.shape
    yp = x_prompt.reshape(pb * pl_, D_MODEL)
    ys = x_sample.reshape(sb * sl, D_MODEL)
    past = cache_fox_k.shape[2]
    k_hist = cache_fox_k.reshape(DEPTH, sb, past * FOX_HEADS, FOX_HD)
    v_hist = cache_fox_v.reshape(DEPTH, sb, past * FOX_HEADS, FOX_HD)
    g0 = g_mix_pre[0].reshape(1, -1)
    hp = _norm(yp, g0, 512)
    hs = _norm(ys, g0, sb * sl)
    kv_p = kv_s = None
    logf_p, logf_s, state_p, state_s = [], [], [], []
    for l in range(DEPTH):
        wts = _layer_weights(l, g_mix_pre, w_in, w_gla_gate_up, b_gla_gate, b_fox_f, g_gla_onorm,
                             g_mix_post, g_mlp_pre, g_mlp_post)
        g_next = g_mix_pre[l + 1].reshape(1, -1) if l + 1 < DEPTH else None
        yp, hp, kv_p, lf, st = _layer(yp, hp, wts, g_next, pb, pl_, l, kv_p, tm_proj=256, tm=512,
                                      gla_rows=1024, gla_chunk=CHUNK,
                                      cast=(w_mlp_up, w_mlp_down, w_out))
        logf_p.append(lf)
        state_p.append(st)
        ys, hs, kv_s, lf, st = _layer(ys, hs, wts, g_next, sb, sl, l, kv_s, tm_proj=sb * sl,
                                      tm=sb * sl, gla_rows=sl, gla_chunk=sl, s0=state_gla[l],
                                      cache=(k_hist, v_hist, cache_fox_logf[l]))
        logf_s.append(lf)
        state_s.append(st)
    kv5 = lambda a, b, s: a.reshape(DEPTH, b, s, FOX_HEADS, FOX_HD)
    return (yp.reshape(x_prompt.shape), ys.reshape(x_sample.shape),
            kv5(kv_p[0], pb, pl_), kv5(kv_p[1], pb, pl_), jnp.stack(logf_p), jnp.stack(state_p),
            kv5(kv_s[0], sb, sl), kv5(kv_s[1], sb, sl), jnp.stack(logf_s), jnp.stack(state_s))
```

```python
import functools

import jax
import jax.numpy as jnp
from jax import lax
from jax.experimental import pallas as pl
from jax.experimental.pallas import tpu as pltpu

F32 = jnp.float32
BF16 = jnp.bfloat16
HIGHEST = lax.Precision.HIGHEST

D_MODEL = 2048
DEPTH = 2
CHUNK = 64
GLA_HEADS = 4
GLA_DK = 128
GLA_DV = 256
GLA_QK = GLA_HEADS * GLA_DK
GLA_WIDTH = GLA_HEADS * GLA_DV
GLA_RANK = 16
GLA_TAU = 16.0
FOX_HEADS = 8
FOX_HD = 128
FOX_WIDTH = FOX_HEADS * FOX_HD
D_FF = 4 * D_MODEL
EPS = 1e-6
NEG = -1e30

O_GLR = 2 * GLA_QK + GLA_WIDTH
O_RG = O_GLR + GLA_RANK
O_FL = O_RG + GLA_WIDTH + 3 * FOX_WIDTH
PROJ_WIDTH = O_FL + FOX_HEADS
MAIN_WIDTH = PROJ_WIDTH - GLA_RANK - FOX_HEADS
LANES = 128
SMALL_FL = GLA_RANK

VMEM_LIMIT = 52 * 1024 * 1024
NT_DIMS = (((1,), (1,)), ((), ()))
TN_DIMS = (((0,), (0,)), ((), ()))


def _log_sigmoid(x):
    return jnp.minimum(x, 0.0) - jnp.log1p(jnp.exp(-jnp.abs(x)))


def _rms(x):
    return x * lax.rsqrt(jnp.mean(x * x, axis=-1, keepdims=True) + EPS)


def _lane_fold(x, op):
    parts = [x[:, i * LANES:(i + 1) * LANES] for i in range(x.shape[1] // LANES)]
    while len(parts) > 1:
        parts = [op(parts[i], parts[i + 1]) for i in range(0, len(parts), 2)]
    return parts[0]


def _norm_kernel(x_ref, g_ref, h_ref):
    h_ref[...] = (_rms(x_ref[...]) * g_ref[...]).astype(BF16)


def _norm(x, g, tm):
    m = x.shape[0]
    return pl.pallas_call(
        _norm_kernel,
        grid=(m // tm,),
        in_specs=[pl.BlockSpec((tm, D_MODEL), lambda i: (i, 0)),
                  pl.BlockSpec((1, D_MODEL), lambda i: (0, 0))],
        out_specs=pl.BlockSpec((tm, D_MODEL), lambda i: (i, 0)),
        out_shape=jax.ShapeDtypeStruct((m, D_MODEL), BF16),
        compiler_params=pltpu.CompilerParams(
            dimension_semantics=("arbitrary",), vmem_limit_bytes=VMEM_LIMIT),
        name="norm",
    )(x, g)


CAST_ROWS = 512


def _cast_kernel(w_ref, o_ref):
    o_ref[...] = w_ref[...].astype(BF16)


def _cast_w_in(w_t, layer):
    return pl.pallas_call(
        _cast_kernel,
        grid=(pl.cdiv(PROJ_WIDTH, CAST_ROWS),),
        in_specs=[pl.BlockSpec((None, CAST_ROWS, D_MODEL), lambda i: (layer, i, 0))],
        out_specs=pl.BlockSpec((CAST_ROWS, D_MODEL), lambda i: (i, 0)),
        out_shape=jax.ShapeDtypeStruct((PROJ_WIDTH, D_MODEL), BF16),
        compiler_params=pltpu.CompilerParams(
            dimension_semantics=("arbitrary",), vmem_limit_bytes=VMEM_LIMIT),
        name="cast_w_in",
    )(w_t)


IN_TN = 512
HEADS_PER_TN = IN_TN // FOX_HD
GLA_BLOCKS = O_GLR // IN_TN
N_IN = 7
N_OUT = 12
IN_PROJ_VMEM_LIMIT = 58 * 1024 * 1024


def _in_proj_kernel(*refs, nb, lt, n_alias, n_cast):
    h_ref, wt_ref, ws_ref, wgu_ref, bg_ref, bf_ref, bft_ref = refs[:N_IN]
    cast_in = refs[N_IN:N_IN + n_cast]
    outs = refs[N_IN + n_cast + n_alias:]
    (qg_ref, kg_ref, vg_ref, rg_ref, gate_ref, logf_ref, logft_ref,
     qh_ref, kh_ref, vh_ref, kf_ref, vf_ref) = outs[:N_OUT]
    tm = nb * lt
    h = h_ref[...]

    for src, dst in zip(cast_in, outs[N_OUT:]):
        dst[...] = src[...].astype(BF16)

    zs = lax.dot_general(h, ws_ref[...], NT_DIMS, preferred_element_type=F32)
    zst = lax.dot_general(ws_ref[...], h, NT_DIMS, preferred_element_type=F32)
    gpre = jnp.dot(zs.astype(BF16), wgu_ref[...], preferred_element_type=F32) + bg_ref[...]
    gate_ref[...] = _log_sigmoid(gpre) / GLA_TAU
    logf_ref[...] = _log_sigmoid(zs[:, SMALL_FL:SMALL_FL + FOX_HEADS] + bf_ref[...])
    logft_ref[...] = _log_sigmoid(zst[SMALL_FL:SMALL_FL + FOX_HEADS, :] + bft_ref[...])

    def mm(cb):
        r0 = cb * IN_TN + (GLA_RANK if cb >= GLA_BLOCKS else 0)
        return lax.dot_general(h, wt_ref[r0:r0 + IN_TN, :], NT_DIMS, preferred_element_type=F32)

    def heads(ref, val, half):
        for hh in range(HEADS_PER_TN):
            ref[:, half * HEADS_PER_TN + hh] = (
                val[:, hh * FOX_HD:(hh + 1) * FOX_HD].astype(BF16).reshape(nb, lt, FOX_HD))

    def rows(ref, val, half):
        for hh in range(HEADS_PER_TN):
            ref[pl.ds(half * HEADS_PER_TN + hh, tm, stride=FOX_HEADS), :] = (
                val[:, hh * FOX_HD:(hh + 1) * FOX_HD])

    qg_ref[...] = mm(0)
    kg_ref[...] = mm(1)
    for half in range(2):
        cols = slice(half * IN_TN, (half + 1) * IN_TN)
        vg_ref[:, cols] = mm(2 + half).astype(BF16)
        rg_ref[:, cols] = mm(4 + half)
        heads(qh_ref, mm(6 + half) * (FOX_HD ** -0.5), half)
        val = mm(8 + half)
        heads(kh_ref, val, half)
        rows(kf_ref, val, half)
        val = mm(10 + half)
        heads(vh_ref, val, half)
        rows(vf_ref, val, half)


def _in_proj(h, wts, batch, seq, tm, layer, kv_prev, cast=()):
    m = h.shape[0]
    steps = m // tm
    lt = min(tm, seq)
    nb = tm // lt
    nt = seq // lt
    const = lambda i: (0, 0)
    rowblk = lambda w: pl.BlockSpec((tm, w), lambda i: (i, 0))
    head_shape = jax.ShapeDtypeStruct((batch, FOX_HEADS, seq, FOX_HD), BF16)
    head_spec = pl.BlockSpec((nb, FOX_HEADS, lt, FOX_HD), lambda i: (i // nt, 0, i % nt, 0))
    kv_shape = jax.ShapeDtypeStruct((DEPTH, m * FOX_HEADS, FOX_HD), F32)
    kv_spec = pl.BlockSpec((None, tm * FOX_HEADS, FOX_HD), lambda i: (layer, i, 0))
    out_shape = [
        jax.ShapeDtypeStruct((m, GLA_QK), F32), jax.ShapeDtypeStruct((m, GLA_QK), F32),
        jax.ShapeDtypeStruct((m, GLA_WIDTH), BF16), jax.ShapeDtypeStruct((m, GLA_WIDTH), F32),
        jax.ShapeDtypeStruct((m, GLA_QK), F32), jax.ShapeDtypeStruct((m, FOX_HEADS), F32),
        jax.ShapeDtypeStruct((FOX_HEADS, m), F32),
        head_shape, head_shape, head_shape, kv_shape, kv_shape,
    ]
    out_specs = [
        rowblk(GLA_QK), rowblk(GLA_QK), rowblk(GLA_WIDTH), rowblk(GLA_WIDTH),
        rowblk(GLA_QK), rowblk(FOX_HEADS),
        pl.BlockSpec((FOX_HEADS, tm), lambda i: (0, i)),
        head_spec, head_spec, head_spec, kv_spec, kv_spec,
    ]
    assert len(out_specs) == N_OUT
    in_specs = [
        rowblk(D_MODEL),
        pl.BlockSpec((PROJ_WIDTH, D_MODEL), const),
        pl.BlockSpec((LANES, D_MODEL), const),
        pl.BlockSpec((LANES, GLA_QK), const),
        pl.BlockSpec((1, GLA_QK), const),
        pl.BlockSpec((1, FOX_HEADS), const),
        pl.BlockSpec((FOX_HEADS, 1), const),
    ]
    args = [h, wts["w_t"], wts["w_small_t"], wts["w_gu"], wts["b_gate"], wts["b_f"], wts["b_ft"]]
    assert len(args) == N_IN
    for w in cast:
        _, r, c = w.shape
        in_specs.append(pl.BlockSpec((None, r // steps, c), lambda i: (layer, i, 0)))
        out_specs.append(pl.BlockSpec((r // steps, c), lambda i: (i, 0)))
        out_shape.append(jax.ShapeDtypeStruct((r, c), BF16))
        args.append(w)
    aliases = {}
    if kv_prev is not None:
        in_specs += [pl.BlockSpec(memory_space=pl.ANY)] * 2
        aliases = {len(args): N_OUT - 2, len(args) + 1: N_OUT - 1}
        args += list(kv_prev)
    return pl.pallas_call(
        functools.partial(_in_proj_kernel, nb=nb, lt=lt, n_alias=len(aliases), n_cast=len(cast)),
        grid=(steps,),
        in_specs=in_specs,
        out_specs=out_specs,
        out_shape=out_shape,
        input_output_aliases=aliases,
        compiler_params=pltpu.CompilerParams(
            dimension_semantics=("arbitrary",), vmem_limit_bytes=IN_PROJ_VMEM_LIMIT),
        name="in_proj",
    )(*args)


def _gla_kernel(*refs, chunk, nchunks, hps, has_s0):
    if has_s0:
        q_ref, k_ref, v_ref, g_ref, r_ref, gon_ref, s0_ref, o_ref, sout_ref, st = refs
    else:
        q_ref, k_ref, v_ref, g_ref, r_ref, gon_ref, o_ref, sout_ref, st = refs
    t = pl.program_id(2)

    @pl.when(t == 0)
    def _():
        for hh in range(hps):
            st[hh] = s0_ref[hh].T if has_s0 else jnp.zeros((GLA_DV, GLA_DK), F32)

    row = lax.broadcasted_iota(jnp.int32, (chunk, chunk), 0)
    col = lax.broadcasted_iota(jnp.int32, (chunk, chunk), 1)
    causal = col <= row
    tri = causal.astype(BF16)
    qscale = GLA_DK ** -0.5
    items = [(hh, c) for hh in range(hps) for c in range(nchunks)]
    sl = lambda c: pl.ds(c * chunk, chunk)
    dk = lambda hh: slice(hh * GLA_DK, (hh + 1) * GLA_DK)
    dv = lambda hh: slice(hh * GLA_DV, (hh + 1) * GLA_DV)

    def cumsum(g):
        b = None
        for _ in range(3):
            piece = g.astype(BF16)
            part = jnp.dot(tri, piece, preferred_element_type=F32)
            b = part if b is None else b + part
            g = g - piece.astype(F32)
        return b

    bs = [cumsum(g_ref[sl(c), dk(hh)]) for hh, c in items]
    b_lasts = [b[chunk - 1:chunk, :] for b in bs]
    q_is = [(q_ref[sl(c), dk(hh)] * qscale * jnp.exp(bs[n])).astype(BF16)
            for n, (hh, c) in enumerate(items)]
    k_is = [(k_ref[sl(c), dk(hh)] * jnp.exp(-bs[n])).astype(BF16) for n, (hh, c) in enumerate(items)]
    k_es = [(k_ref[sl(c), dk(hh)] * jnp.exp(b_lasts[n] - bs[n])).astype(BF16)
            for n, (hh, c) in enumerate(items)]
    atts = [lax.dot_general(q_is[n], k_is[n], NT_DIMS, preferred_element_type=F32)
            for n in range(len(items))]
    kv_ts = [lax.dot_general(v_ref[sl(c), dv(hh)], k_es[n], TN_DIMS, preferred_element_type=F32)
             for n, (hh, c) in enumerate(items)]
    atts = [jnp.where(causal, a, 0.0).astype(BF16) for a in atts]
    o_intra = [jnp.dot(atts[n], v_ref[sl(c), dv(hh)], preferred_element_type=F32)
               for n, (hh, c) in enumerate(items)]
    states = {}
    for n, (hh, c) in enumerate(items):
        prev = st[hh] if c == 0 else states[hh, c]
        states[hh, c] = prev
        states[hh, c + 1] = prev * jnp.exp(b_lasts[n]) + kv_ts[n]
    o_inter = [lax.dot_general(q_is[n], states[hh, c].astype(BF16), NT_DIMS,
                               preferred_element_type=F32) for n, (hh, c) in enumerate(items)]
    for n, (hh, c) in enumerate(items):
        r = r_ref[sl(c), dv(hh)]
        o = o_intra[n] + o_inter[n]
        o_ref[sl(c), dv(hh)] = (
            _rms(o) * gon_ref[:, dv(hh)] * (r * jax.nn.sigmoid(r))).astype(o_ref.dtype)
    for hh in range(hps):
        st[hh] = states[hh, nchunks]

    @pl.when(t == pl.num_programs(2) - 1)
    def _():
        for hh in range(hps):
            sout_ref[hh] = states[hh, nchunks].T


def _gla(qg, kg, vg, gate, rg, g_onorm, s0, batch, seq, rows, chunk, hps):
    has_s0 = s0 is not None
    nchunks = rows // chunk
    r3 = lambda a: a.reshape(batch, seq, a.shape[-1])
    tok = lambda w: pl.BlockSpec((None, rows, hps * w), lambda b, h, t: (b, t, h))
    state_spec = pl.BlockSpec((None, hps, GLA_DK, GLA_DV), lambda b, h, t: (b, h, 0, 0))
    in_specs = [tok(GLA_DK), tok(GLA_DK), tok(GLA_DV), tok(GLA_DK), tok(GLA_DV),
                pl.BlockSpec((1, hps * GLA_DV), lambda b, h, t: (0, h))]
    args = [r3(qg), r3(kg), r3(vg), r3(gate), r3(rg), g_onorm]
    if has_s0:
        in_specs.append(state_spec)
        args.append(s0)
    return pl.pallas_call(
        functools.partial(_gla_kernel, chunk=chunk, nchunks=nchunks, hps=hps, has_s0=has_s0),
        grid=(batch, GLA_HEADS // hps, seq // rows),
        in_specs=in_specs,
        out_specs=[tok(GLA_DV), state_spec],
        out_shape=[jax.ShapeDtypeStruct((batch, seq, GLA_WIDTH), BF16),
                   jax.ShapeDtypeStruct((batch, GLA_HEADS, GLA_DK, GLA_DV), F32)],
        scratch_shapes=[pltpu.VMEM((hps, GLA_DV, GLA_DK), F32)],
        compiler_params=pltpu.CompilerParams(
            dimension_semantics=("arbitrary", "arbitrary", "arbitrary"), vmem_limit_bytes=VMEM_LIMIT),
        name="gla",
    )(*args)


def _cumsum_kernel(x_ref, o_ref, *, nblk):
    x = x_ref[...]
    length = nblk * LANES
    r = lax.broadcasted_iota(jnp.int32, (LANES, LANES), 0)
    c = lax.broadcasted_iota(jnp.int32, (LANES, LANES), 1)
    triu = (r <= c).astype(F32)
    li = lax.broadcasted_iota(jnp.int32, (length, LANES), 0)
    ji = lax.broadcasted_iota(jnp.int32, (length, LANES), 1)
    before = (li < ji * LANES).astype(F32)
    offs = jnp.dot(x, before, precision=HIGHEST, preferred_element_type=F32)
    for j in range(nblk):
        blk = x[:, j * LANES:(j + 1) * LANES]
        o_ref[:, j * LANES:(j + 1) * LANES] = (
            jnp.dot(blk, triu, precision=HIGHEST, preferred_element_type=F32) + offs[:, j:j + 1])


def _cumsum(xt):
    spec = pl.BlockSpec(xt.shape, lambda i: (0, 0))
    return pl.pallas_call(
        functools.partial(_cumsum_kernel, nblk=xt.shape[1] // LANES),
        grid=(1,),
        in_specs=[spec],
        out_specs=spec,
        out_shape=jax.ShapeDtypeStruct(xt.shape, F32),
        compiler_params=pltpu.CompilerParams(dimension_semantics=("arbitrary",)),
        name="cumsum",
    )(xt)


def _pick_row(block, h):
    sel = lax.broadcasted_iota(jnp.int32, (FOX_HEADS, 1), 0) == h
    return jnp.sum(jnp.where(sel, block, 0.0), axis=0, keepdims=True)


def _pick_col(block, h):
    sel = lax.broadcasted_iota(jnp.int32, (1, FOX_HEADS), 1) == h
    return jnp.sum(jnp.where(sel, block, 0.0), axis=1, keepdims=True)


def _fox_prompt_kernel(q_ref, k_ref, v_ref, crow_ref, ccol_ref, o_ref, s_scr, *, tq, tk, nq):
    h = pl.program_id(1)
    crow = _pick_row(crow_ref[...], h)
    ccol = _pick_col(ccol_ref[...], h)
    rows = lambda qi: slice(qi * tq, (qi + 1) * tq)

    def pieces(qi):
        end = (qi + 1) * tq
        return [(c0, min(c0 + tk, end)) for c0 in range(0, end, tk)]

    def pass1(qi):
        q = q_ref[rows(qi), :]
        cq = ccol[rows(qi), :]
        mx = None
        ps = pieces(qi)
        for n, (c0, c1) in enumerate(ps):
            s = lax.dot_general(q, k_ref[c0:c1, :], NT_DIMS, preferred_element_type=F32)
            s = s + (cq - crow[:, c0:c1])
            if n == len(ps) - 1:
                qpos = lax.broadcasted_iota(jnp.int32, s.shape, 0) + qi * tq
                kpos = lax.broadcasted_iota(jnp.int32, s.shape, 1) + c0
                s = jnp.where(kpos <= qpos, s, NEG)
            s_scr[qi, n, :, :c1 - c0] = s
            part = _lane_fold(s, jnp.maximum)
            mx = part if mx is None else jnp.maximum(mx, part)
        return jnp.max(mx, axis=1, keepdims=True)

    def pass2(qi, m):
        lsum = acc = None
        for n, (c0, c1) in enumerate(pieces(qi)):
            p = jnp.exp(s_scr[qi, n, :, :c1 - c0] - m)
            pv = jnp.dot(p.astype(BF16), v_ref[c0:c1, :], preferred_element_type=F32)
            part = _lane_fold(p, jnp.add)
            lsum = part if lsum is None else lsum + part
            acc = pv if acc is None else acc + pv
        o_ref[rows(qi), :] = (acc / jnp.sum(lsum, axis=1, keepdims=True)).astype(o_ref.dtype)

    m_prev = pass1(0)
    for qi in range(1, nq):
        m_next = pass1(qi)
        pass2(qi - 1, m_prev)
        m_prev = m_next
    pass2(nq - 1, m_prev)


def _fox_prompt(qh, kh, vh, c_row, c_col, batch, seq, tq, tk):
    nq = seq // tq
    whole = pl.BlockSpec((None, None, seq, FOX_HD), lambda b, h: (b, h, 0, 0))
    return pl.pallas_call(
        functools.partial(_fox_prompt_kernel, tq=tq, tk=tk, nq=nq),
        grid=(batch, FOX_HEADS),
        in_specs=[
            whole, whole, whole,
            pl.BlockSpec((FOX_HEADS, seq), lambda b, h: (b, 0)),
            pl.BlockSpec((None, seq, FOX_HEADS), lambda b, h: (b, 0, 0)),
        ],
        out_specs=pl.BlockSpec((None, seq, FOX_HD), lambda b, h: (b, 0, h)),
        out_shape=jax.ShapeDtypeStruct((batch, seq, FOX_WIDTH), BF16),
        scratch_shapes=[pltpu.VMEM((nq, seq // tk, tq, tk), F32)],
        compiler_params=pltpu.CompilerParams(
            dimension_semantics=("arbitrary", "arbitrary"), vmem_limit_bytes=VMEM_LIMIT),
        name="fox_prompt",
    )(qh, kh, vh, c_row, c_col)


FOX_TKC = 512


def _fox_sample_kernel(q_ref, kc_ref, vc_ref, kn_ref, vn_ref, cc_ref, cn_ref, cq_ref, o_ref,
                       s_scr, v_scr, mx_scr, *, new, nkc):
    c = pl.program_id(1)
    q = q_ref[...]
    cq = cq_ref[...]

    def logits(k, ck):
        s = jnp.einsum('hqd,hkd->hqk', q, k, preferred_element_type=F32)
        return s + (cq - ck)

    def head_rows(ref):
        return jnp.stack([ref[pl.ds(h, FOX_TKC, stride=FOX_HEADS), :]
                          for h in range(FOX_HEADS)]).astype(BF16)

    def lane_max(s):
        parts = [s[:, :, i * LANES:(i + 1) * LANES] for i in range(s.shape[2] // LANES)]
        while len(parts) > 1:
            parts = [jnp.maximum(parts[i], parts[i + 1]) for i in range(0, len(parts), 2)]
        return parts[0]

    s = logits(head_rows(kc_ref), cc_ref[...])
    s_scr[c] = s
    v_scr[c] = head_rows(vc_ref)
    part = lane_max(s)

    @pl.when(c == 0)
    def _():
        mx_scr[...] = part

    @pl.when(c > 0)
    def _():
        mx_scr[...] = jnp.maximum(mx_scr[...], part)

    @pl.when(c == nkc - 1)
    def _():
        row = lax.broadcasted_iota(jnp.int32, (new, new), 0)
        col = lax.broadcasted_iota(jnp.int32, (new, new), 1)
        s_new = jnp.where((col <= row)[None], logits(kn_ref[...], cn_ref[:, :, :new]), NEG)
        m = jnp.maximum(jnp.max(mx_scr[...], axis=2, keepdims=True),
                        jnp.max(s_new, axis=2, keepdims=True))
        p = jnp.exp(s_new - m)
        l = jnp.sum(p, axis=2, keepdims=True)
        acc = jnp.einsum('hqk,hkd->hqd', p.astype(BF16), vn_ref[...], preferred_element_type=F32)
        for cc in range(nkc):
            p = jnp.exp(s_scr[cc] - m)
            l = l + jnp.sum(p, axis=2, keepdims=True)
            acc = acc + jnp.einsum('hqk,hkd->hqd', p.astype(BF16), v_scr[cc],
                                   preferred_element_type=F32)
        out = acc / l
        for h in range(FOX_HEADS):
            o_ref[:, h * FOX_HD:(h + 1) * FOX_HD] = out[h].astype(o_ref.dtype)


def _fox_sample(qh, k_hist, v_hist, layer, kh, vh, c_row, c_q, batch, new):
    past = k_hist.shape[2] // FOX_HEADS
    nkc = past // FOX_TKC
    heads_new = pl.BlockSpec((None, FOX_HEADS, new, FOX_HD), lambda b, c: (b, 0, 0, 0))
    hist = pl.BlockSpec((None, None, FOX_TKC * FOX_HEADS, FOX_HD), lambda b, c: (layer, b, c, 0))
    return pl.pallas_call(
        functools.partial(_fox_sample_kernel, new=new, nkc=nkc),
        grid=(batch, nkc),
        in_specs=[heads_new, hist, hist, heads_new, heads_new,
                  pl.BlockSpec((FOX_HEADS, 1, FOX_TKC), lambda b, c: (b, 0, c)),
                  pl.BlockSpec((FOX_HEADS, 1, FOX_TKC), lambda b, c: (b, 0, nkc)),
                  pl.BlockSpec((None, FOX_HEADS, new, 1), lambda b, c: (b, 0, 0, 0))],
        out_specs=pl.BlockSpec((None, new, FOX_WIDTH), lambda b, c: (b, 0, 0)),
        out_shape=jax.ShapeDtypeStruct((batch, new, FOX_WIDTH), BF16),
        scratch_shapes=[pltpu.VMEM((nkc, FOX_HEADS, new, FOX_TKC), F32),
                        pltpu.VMEM((nkc, FOX_HEADS, FOX_TKC, FOX_HD), BF16),
                        pltpu.VMEM((FOX_HEADS, new, LANES), F32)],
        compiler_params=pltpu.CompilerParams(
            dimension_semantics=("arbitrary", "arbitrary"), vmem_limit_bytes=VMEM_LIMIT),
        name="fox_sample",
    )(qh, k_hist, v_hist, kh, vh, c_row[:, None, :], c_row[:, None, :], c_q)


OUT_PARTS = 2


def _out_proj_kernel(og_ref, of_ref, w_ref, x_ref, gpost_ref, gpre_ref, x1_ref, h2_ref):
    rows = og_ref.shape[0] // OUT_PARTS
    part = lambda a: slice(a * rows, (a + 1) * rows)
    ys = []
    for a in range(OUT_PARTS):
        y = jnp.dot(og_ref[part(a), :], w_ref[:GLA_WIDTH, :], preferred_element_type=F32)
        ys.append(y + jnp.dot(of_ref[part(a), :], w_ref[GLA_WIDTH:, :], preferred_element_type=F32))
    for a in range(OUT_PARTS):
        x1 = x_ref[part(a), :] + _rms(ys[a]) * gpost_ref[...]
        x1_ref[part(a), :] = x1
        h2_ref[part(a), :] = (_rms(x1) * gpre_ref[...]).astype(BF16)


def _out_proj(o_gla, o_fox, w_out, x, g_post, g_mlp_pre, tm):
    m = x.shape[0]
    const = lambda i: (0, 0)
    rowblk = lambda w: pl.BlockSpec((tm, w), lambda i: (i, 0))
    return pl.pallas_call(
        _out_proj_kernel,
        grid=(m // tm,),
        in_specs=[rowblk(GLA_WIDTH), rowblk(FOX_WIDTH),
                  pl.BlockSpec((D_MODEL, D_MODEL), const),
                  rowblk(D_MODEL),
                  pl.BlockSpec((1, D_MODEL), const), pl.BlockSpec((1, D_MODEL), const)],
        out_specs=[rowblk(D_MODEL), rowblk(D_MODEL)],
        out_shape=[jax.ShapeDtypeStruct((m, D_MODEL), F32), jax.ShapeDtypeStruct((m, D_MODEL), BF16)],
        compiler_params=pltpu.CompilerParams(
            dimension_semantics=("arbitrary",), vmem_limit_bytes=VMEM_LIMIT),
        name="out_proj",
    )(o_gla.reshape(m, GLA_WIDTH), o_fox.reshape(m, FOX_WIDTH), w_out, x, g_post, g_mlp_pre)


MLP_TF = 1024
MLP_TN = 512


def _mlp_kernel(*refs, has_next):
    if has_next:
        h_ref, wu_ref, wd_ref, x1_ref, g_ref, gnext_ref, o_ref, hn_ref, u_scr = refs
    else:
        h_ref, wu_ref, wd_ref, x1_ref, g_ref, o_ref, u_scr = refs
    j = pl.program_id(1)

    def step(first):
        u = jnp.maximum(jnp.dot(h_ref[...], wu_ref[...], preferred_element_type=F32), 0.0)
        u_scr[...] = (u * u).astype(BF16)
        for n in range(0, D_MODEL, MLP_TN):
            part = jnp.dot(u_scr[...], wd_ref[:, n:n + MLP_TN], preferred_element_type=F32)
            if first:
                o_ref[:, n:n + MLP_TN] = part
            else:
                o_ref[:, n:n + MLP_TN] += part

    pl.when(j == 0)(functools.partial(step, True))
    pl.when(j > 0)(functools.partial(step, False))

    @pl.when(j == pl.num_programs(1) - 1)
    def _():
        x2 = x1_ref[...] + _rms(o_ref[...]) * g_ref[...]
        o_ref[...] = x2
        if has_next:
            hn_ref[...] = (_rms(x2) * gnext_ref[...]).astype(BF16)


def _mlp(h2, w_up, w_down, x1, g_post, g_next, tm):
    m = x1.shape[0]
    has_next = g_next is not None
    rowblk = pl.BlockSpec((tm, D_MODEL), lambda i, j: (i, 0))
    gain = pl.BlockSpec((1, D_MODEL), lambda i, j: (0, 0))
    in_specs = [rowblk,
                pl.BlockSpec((D_MODEL, MLP_TF), lambda i, j: (0, j)),
                pl.BlockSpec((MLP_TF, D_MODEL), lambda i, j: (j, 0)),
                rowblk, gain]
    args = [h2, w_up, w_down, x1, g_post]
    out_specs = [rowblk]
    out_shape = [jax.ShapeDtypeStruct((m, D_MODEL), F32)]
    if has_next:
        in_specs.append(gain)
        args.append(g_next)
        out_specs.append(rowblk)
        out_shape.append(jax.ShapeDtypeStruct((m, D_MODEL), BF16))
    outs = pl.pallas_call(
        functools.partial(_mlp_kernel, has_next=has_next),
        grid=(m // tm, D_FF // MLP_TF),
        in_specs=in_specs,
        out_specs=out_specs,
        out_shape=out_shape,
        scratch_shapes=[pltpu.VMEM((tm, MLP_TF), BF16)],
        compiler_params=pltpu.CompilerParams(
            dimension_semantics=("arbitrary", "arbitrary"), vmem_limit_bytes=VMEM_LIMIT),
        name="mlp",
    )(*args)
    return (outs[0], outs[1]) if has_next else (outs[0], None)


def _layer_weights(l, g_mix_pre, w_in, w_gla_gate_up, b_gla_gate, b_fox_f, g_gla_onorm,
                   g_mix_post, g_mlp_pre, g_mlp_post):
    row = lambda v: v.reshape(1, -1)
    w_t = _cast_w_in(jnp.swapaxes(w_in, 1, 2), l)
    w_small_t = jnp.concatenate([w_t[O_GLR:O_RG], w_t[O_FL:]], axis=0)
    return dict(
        g_pre=row(g_mix_pre[l]),
        w_t=w_t,
        w_small_t=jnp.pad(w_small_t, ((0, LANES - GLA_RANK - FOX_HEADS), (0, 0))),
        w_gu=jnp.pad(w_gla_gate_up[l], ((0, LANES - GLA_RANK), (0, 0))).astype(BF16),
        b_gate=row(b_gla_gate[l]),
        b_f=row(b_fox_f[l]),
        b_ft=b_fox_f[l].reshape(-1, 1),
        g_onorm=row(g_gla_onorm[l]),
        g_post=row(g_mix_post[l]),
        g_mlp_pre=row(g_mlp_pre[l]),
        g_mlp_post=row(g_mlp_post[l]),
    )


def _layer(x, h, wts, g_next, batch, seq, layer, kv_prev, tm_proj, tm, gla_rows, gla_chunk,
           s0=None, cache=None, cast=None):
    outs = _in_proj(h, wts, batch, seq, tm_proj, layer, kv_prev, cast or ())
    qg, kg, vg, rg, gate, logf, logft, qh, kh, vh, kf, vf = outs[:N_OUT]
    if cast:
        wts["w_up"], wts["w_down"], wts["w_out"] = outs[N_OUT:]
    hps = GLA_HEADS if gla_rows == gla_chunk else 1
    o_gla, s_new = _gla(qg, kg, vg, gate, rg, wts["g_onorm"], s0, batch, seq, gla_rows, gla_chunk, hps)
    if cache is None:
        lf = logft.reshape(FOX_HEADS, batch, seq).transpose(1, 0, 2)
        c_row = _cumsum(lf.reshape(batch * FOX_HEADS, seq))
        c_col = c_row.reshape(batch, FOX_HEADS, seq).transpose(0, 2, 1)
        o_fox = _fox_prompt(qh, kh, vh, c_row, c_col, batch, seq, tq=256, tk=512)
    else:
        k_hist, v_hist, logf_cache = cache
        past = logf_cache.shape[1]
        total = past + seq
        seg = past + FOX_TKC
        lf = jnp.concatenate([logf_cache, logf.reshape(batch, seq, FOX_HEADS)], axis=1)
        lf = jnp.pad(lf.transpose(0, 2, 1), ((0, 0), (0, 0), (0, seg - total)))
        c_row = _cumsum(lf.reshape(batch * FOX_HEADS, seg))
        c_q = c_row.reshape(batch, FOX_HEADS, seg)[:, :, past:total][..., None]
        o_fox = _fox_sample(qh, k_hist, v_hist, layer, kh, vh, c_row, c_q, batch, seq)
    x1, h2 = _out_proj(o_gla, o_fox, wts["w_out"], x, wts["g_post"], wts["g_mlp_pre"], tm)
    x2, h_next = _mlp(h2, wts["w_up"], wts["w_down"], x1, wts["g_mlp_post"], g_next, tm)
    return x2, h_next, (kf, vf), logf.reshape(batch, seq, FOX_HEADS), s_new


def kernel(x_prompt, x_sample, cache_fox_k, cache_fox_v, cache_fox_logf, state_gla, g_mix_pre, w_in,
           w_gla_gate_up, b_gla_gate, b_fox_f, g_gla_onorm, w_out, g_mix_post, g_mlp_pre, w_mlp_up,
           w_mlp_down, g_mlp_post):
    pb, pl_, _ = x_prompt.shape
    sb, sl, _ = x_sample.shape
    yp = x_prompt.reshape(pb * pl_, D_MODEL)
    ys = x_sample.reshape(sb * sl, D_MODEL)
    past = cache_fox_k.shape[2]
    k_hist = cache_fox_k.reshape(DEPTH, sb, past * FOX_HEADS, FOX_HD)
    v_hist = cache_fox_v.reshape(DEPTH, sb, past * FOX_HEADS, FOX_HD)
    g0 = g_mix_pre[0].reshape(1, -1)
    hp = _norm(yp, g0, 512)
    hs = _norm(ys, g0, sb * sl)
    kv_p = kv_s = None
    logf_p, logf_s, state_p, state_s = [], [], [], []
    for l in range(DEPTH):
        wts = _layer_weights(l, g_mix_pre, w_in, w_gla_gate_up, b_gla_gate, b_fox_f, g_gla_onorm,
                             g_mix_post, g_mlp_pre, g_mlp_post)
        g_next = g_mix_pre[l + 1].reshape(1, -1) if l + 1 < DEPTH else None
        yp, hp, kv_p, lf, st = _layer(yp, hp, wts, g_next, pb, pl_, l, kv_p, tm_proj=256, tm=512,
                                      gla_rows=1024, gla_chunk=CHUNK,
                                      cast=(w_mlp_up, w_mlp_down, w_out))
        logf_p.append(lf)
        state_p.append(st)
        ys, hs, kv_s, lf, st = _layer(ys, hs, wts, g_next, sb, sl, l, kv_s, tm_proj=sb * sl,
                                      tm=sb * sl, gla_rows=sl, gla_chunk=sl, s0=state_gla[l],
                                      cache=(k_hist, v_hist, cache_fox_logf[l]))
        logf_s.append(lf)
        state_s.append(st)
    kv5 = lambda a, b, s: a.reshape(DEPTH, b, s, FOX_HEADS, FOX_HD)
    return (yp.reshape(x_prompt.shape), ys.reshape(x_sample.shape),
            kv5(kv_p[0], pb, pl_), kv5(kv_p[1], pb, pl_), jnp.stack(logf_p), jnp.stack(state_p),
            kv5(kv_s[0], sb, sl), kv5(kv_s[1], sb, sl), jnp.stack(logf_s), jnp.stack(state_s))
```

```python
import functools

import jax
import jax.numpy as jnp
from jax import lax
from jax.experimental import pallas as pl
from jax.experimental.pallas import tpu as pltpu

F32 = jnp.float32
BF16 = jnp.bfloat16
HIGHEST = lax.Precision.HIGHEST

D_MODEL = 2048
DEPTH = 2
CHUNK = 64
GLA_HEADS = 4
GLA_DK = 128
GLA_DV = 256
GLA_QK = GLA_HEADS * GLA_DK
GLA_WIDTH = GLA_HEADS * GLA_DV
GLA_RANK = 16
GLA_TAU = 16.0
FOX_HEADS = 8
FOX_HD = 128
FOX_WIDTH = FOX_HEADS * FOX_HD
D_FF = 4 * D_MODEL
EPS = 1e-6
LOG2E = 1.4426950408889634
NEG = -1e30

O_GLR = 2 * GLA_QK + GLA_WIDTH
O_RG = O_GLR + GLA_RANK
O_FL = O_RG + GLA_WIDTH + 3 * FOX_WIDTH
PROJ_WIDTH = O_FL + FOX_HEADS
MAIN_WIDTH = PROJ_WIDTH - GLA_RANK - FOX_HEADS
LANES = 128
SMALL_FL = GLA_RANK

VMEM_LIMIT = 52 * 1024 * 1024
NT_DIMS = (((1,), (1,)), ((), ()))
TN_DIMS = (((0,), (0,)), ((), ()))


def _log_sigmoid(x):
    return jnp.minimum(x, 0.0) - jnp.log1p(jnp.exp(-jnp.abs(x)))


def _rms(x):
    return x * lax.rsqrt(jnp.mean(x * x, axis=-1, keepdims=True) + EPS)


def _lane_fold(x, op):
    parts = [x[:, i * LANES:(i + 1) * LANES] for i in range(x.shape[1] // LANES)]
    while len(parts) > 1:
        parts = [op(parts[i], parts[i + 1]) for i in range(0, len(parts), 2)]
    return parts[0]


def _norm_kernel(x_ref, g_ref, h_ref):
    h_ref[...] = (_rms(x_ref[...]) * g_ref[...]).astype(BF16)


def _norm(x, g, tm):
    m = x.shape[0]
    return pl.pallas_call(
        _norm_kernel,
        grid=(m // tm,),
        in_specs=[pl.BlockSpec((tm, D_MODEL), lambda i: (i, 0)),
                  pl.BlockSpec((1, D_MODEL), lambda i: (0, 0))],
        out_specs=pl.BlockSpec((tm, D_MODEL), lambda i: (i, 0)),
        out_shape=jax.ShapeDtypeStruct((m, D_MODEL), BF16),
        compiler_params=pltpu.CompilerParams(
            dimension_semantics=("arbitrary",), vmem_limit_bytes=VMEM_LIMIT),
        name="norm",
    )(x, g)


CAST_ROWS = 512


def _cast_kernel(w_ref, o_ref):
    o_ref[...] = w_ref[...].astype(BF16)


def _cast_w_in(w_t, layer):
    return pl.pallas_call(
        _cast_kernel,
        grid=(pl.cdiv(PROJ_WIDTH, CAST_ROWS),),
        in_specs=[pl.BlockSpec((None, CAST_ROWS, D_MODEL), lambda i: (layer, i, 0))],
        out_specs=pl.BlockSpec((CAST_ROWS, D_MODEL), lambda i: (i, 0)),
        out_shape=jax.ShapeDtypeStruct((PROJ_WIDTH, D_MODEL), BF16),
        compiler_params=pltpu.CompilerParams(
            dimension_semantics=("arbitrary",), vmem_limit_bytes=VMEM_LIMIT),
        name="cast_w_in",
    )(w_t)


IN_TN = 512
HEADS_PER_TN = IN_TN // FOX_HD
GLA_BLOCKS = O_GLR // IN_TN
N_IN = 7
N_OUT = 12
IN_PROJ_VMEM_LIMIT = 58 * 1024 * 1024


def _in_proj_kernel(*refs, nb, lt, n_alias, n_cast):
    h_ref, wt_ref, ws_ref, wgu_ref, bg_ref, bf_ref, bft_ref = refs[:N_IN]
    cast_in = refs[N_IN:N_IN + n_cast]
    outs = refs[N_IN + n_cast + n_alias:]
    (qg_ref, kg_ref, vg_ref, rg_ref, gate_ref, logf_ref, logft_ref,
     qh_ref, kh_ref, vh_ref, kf_ref, vf_ref) = outs[:N_OUT]
    tm = nb * lt
    h = h_ref[...]

    for src, dst in zip(cast_in, outs[N_OUT:]):
        dst[...] = src[...].astype(BF16)

    zs = lax.dot_general(h, ws_ref[...], NT_DIMS, preferred_element_type=F32)
    zst = lax.dot_general(ws_ref[...], h, NT_DIMS, preferred_element_type=F32)
    gpre = jnp.dot(zs.astype(BF16), wgu_ref[...], preferred_element_type=F32) + bg_ref[...]
    gate_ref[...] = _log_sigmoid(gpre) / GLA_TAU
    logf_ref[...] = _log_sigmoid(zs[:, SMALL_FL:SMALL_FL + FOX_HEADS] + bf_ref[...])
    logft_ref[...] = _log_sigmoid(zst[SMALL_FL:SMALL_FL + FOX_HEADS, :] + bft_ref[...])

    def mm(cb):
        r0 = cb * IN_TN + (GLA_RANK if cb >= GLA_BLOCKS else 0)
        return lax.dot_general(h, wt_ref[r0:r0 + IN_TN, :], NT_DIMS, preferred_element_type=F32)

    def heads(ref, val, half):
        for hh in range(HEADS_PER_TN):
            ref[:, half * HEADS_PER_TN + hh] = (
                val[:, hh * FOX_HD:(hh + 1) * FOX_HD].astype(BF16).reshape(nb, lt, FOX_HD))

    def rows(ref, val, half):
        for hh in range(HEADS_PER_TN):
            ref[pl.ds(half * HEADS_PER_TN + hh, tm, stride=FOX_HEADS), :] = (
                val[:, hh * FOX_HD:(hh + 1) * FOX_HD])

    qg_ref[...] = mm(0)
    kg_ref[...] = mm(1)
    for half in range(2):
        cols = slice(half * IN_TN, (half + 1) * IN_TN)
        vg_ref[:, cols] = mm(2 + half).astype(BF16)
        rg_ref[:, cols] = mm(4 + half)
        heads(qh_ref, mm(6 + half) * (FOX_HD ** -0.5 * LOG2E), half)
        val = mm(8 + half)
        heads(kh_ref, val, half)
        rows(kf_ref, val, half)
        val = mm(10 + half)
        heads(vh_ref, val, half)
        rows(vf_ref, val, half)


def _in_proj(h, wts, batch, seq, tm, layer, kv_prev, cast=()):
    m = h.shape[0]
    steps = m // tm
    lt = min(tm, seq)
    nb = tm // lt
    nt = seq // lt
    const = lambda i: (0, 0)
    rowblk = lambda w: pl.BlockSpec((tm, w), lambda i: (i, 0))
    head_shape = jax.ShapeDtypeStruct((batch, FOX_HEADS, seq, FOX_HD), BF16)
    head_spec = pl.BlockSpec((nb, FOX_HEADS, lt, FOX_HD), lambda i: (i // nt, 0, i % nt, 0))
    kv_shape = jax.ShapeDtypeStruct((DEPTH, m * FOX_HEADS, FOX_HD), F32)
    kv_spec = pl.BlockSpec((None, tm * FOX_HEADS, FOX_HD), lambda i: (layer, i, 0))
    out_shape = [
        jax.ShapeDtypeStruct((m, GLA_QK), F32), jax.ShapeDtypeStruct((m, GLA_QK), F32),
        jax.ShapeDtypeStruct((m, GLA_WIDTH), BF16), jax.ShapeDtypeStruct((m, GLA_WIDTH), F32),
        jax.ShapeDtypeStruct((m, GLA_QK), F32), jax.ShapeDtypeStruct((m, FOX_HEADS), F32),
        jax.ShapeDtypeStruct((FOX_HEADS, m), F32),
        head_shape, head_shape, head_shape, kv_shape, kv_shape,
    ]
    out_specs = [
        rowblk(GLA_QK), rowblk(GLA_QK), rowblk(GLA_WIDTH), rowblk(GLA_WIDTH),
        rowblk(GLA_QK), rowblk(FOX_HEADS),
        pl.BlockSpec((FOX_HEADS, tm), lambda i: (0, i)),
        head_spec, head_spec, head_spec, kv_spec, kv_spec,
    ]
    assert len(out_specs) == N_OUT
    in_specs = [
        rowblk(D_MODEL),
        pl.BlockSpec((PROJ_WIDTH, D_MODEL), const),
        pl.BlockSpec((LANES, D_MODEL), const),
        pl.BlockSpec((LANES, GLA_QK), const),
        pl.BlockSpec((1, GLA_QK), const),
        pl.BlockSpec((1, FOX_HEADS), const),
        pl.BlockSpec((FOX_HEADS, 1), const),
    ]
    args = [h, wts["w_t"], wts["w_small_t"], wts["w_gu"], wts["b_gate"], wts["b_f"], wts["b_ft"]]
    assert len(args) == N_IN
    for w in cast:
        _, r, c = w.shape
        in_specs.append(pl.BlockSpec((None, r // steps, c), lambda i: (layer, i, 0)))
        out_specs.append(pl.BlockSpec((r // steps, c), lambda i: (i, 0)))
        out_shape.append(jax.ShapeDtypeStruct((r, c), BF16))
        args.append(w)
    aliases = {}
    if kv_prev is not None:
        in_specs += [pl.BlockSpec(memory_space=pl.ANY)] * 2
        aliases = {len(args): N_OUT - 2, len(args) + 1: N_OUT - 1}
        args += list(kv_prev)
    return pl.pallas_call(
        functools.partial(_in_proj_kernel, nb=nb, lt=lt, n_alias=len(aliases), n_cast=len(cast)),
        grid=(steps,),
        in_specs=in_specs,
        out_specs=out_specs,
        out_shape=out_shape,
        input_output_aliases=aliases,
        compiler_params=pltpu.CompilerParams(
            dimension_semantics=("arbitrary",), vmem_limit_bytes=IN_PROJ_VMEM_LIMIT),
        name="in_proj",
    )(*args)


def _gla_kernel(*refs, chunk, nchunks, hps, has_s0):
    if has_s0:
        q_ref, k_ref, v_ref, g_ref, r_ref, gon_ref, s0_ref, o_ref, sout_ref, st = refs
    else:
        q_ref, k_ref, v_ref, g_ref, r_ref, gon_ref, o_ref, sout_ref, st = refs
    t = pl.program_id(2)

    @pl.when(t == 0)
    def _():
        for hh in range(hps):
            st[hh] = s0_ref[hh].T if has_s0 else jnp.zeros((GLA_DV, GLA_DK), F32)

    row = lax.broadcasted_iota(jnp.int32, (chunk, chunk), 0)
    col = lax.broadcasted_iota(jnp.int32, (chunk, chunk), 1)
    causal = col <= row
    tri = causal.astype(BF16)
    qscale = GLA_DK ** -0.5
    items = [(hh, c) for hh in range(hps) for c in range(nchunks)]
    sl = lambda c: pl.ds(c * chunk, chunk)
    dk = lambda hh: slice(hh * GLA_DK, (hh + 1) * GLA_DK)
    dv = lambda hh: slice(hh * GLA_DV, (hh + 1) * GLA_DV)

    def cumsum(g):
        b = None
        for _ in range(3):
            piece = g.astype(BF16)
            part = jnp.dot(tri, piece, preferred_element_type=F32)
            b = part if b is None else b + part
            g = g - piece.astype(F32)
        return b

    bs = [cumsum(g_ref[sl(c), dk(hh)]) for hh, c in items]
    b_lasts = [b[chunk - 1:chunk, :] for b in bs]
    q_is = [(q_ref[sl(c), dk(hh)] * qscale * jnp.exp(bs[n])).astype(BF16)
            for n, (hh, c) in enumerate(items)]
    k_is = [(k_ref[sl(c), dk(hh)] * jnp.exp(-bs[n])).astype(BF16) for n, (hh, c) in enumerate(items)]
    k_es = [(k_ref[sl(c), dk(hh)] * jnp.exp(b_lasts[n] - bs[n])).astype(BF16)
            for n, (hh, c) in enumerate(items)]
    atts = [lax.dot_general(q_is[n], k_is[n], NT_DIMS, preferred_element_type=F32)
            for n in range(len(items))]
    kv_ts = [lax.dot_general(v_ref[sl(c), dv(hh)], k_es[n], TN_DIMS, preferred_element_type=F32)
             for n, (hh, c) in enumerate(items)]
    atts = [jnp.where(causal, a, 0.0).astype(BF16) for a in atts]
    o_intra = [jnp.dot(atts[n], v_ref[sl(c), dv(hh)], preferred_element_type=F32)
               for n, (hh, c) in enumerate(items)]
    states = {}
    for n, (hh, c) in enumerate(items):
        prev = st[hh] if c == 0 else states[hh, c]
        states[hh, c] = prev
        states[hh, c + 1] = prev * jnp.exp(b_lasts[n]) + kv_ts[n]
    o_inter = [lax.dot_general(q_is[n], states[hh, c].astype(BF16), NT_DIMS,
                               preferred_element_type=F32) for n, (hh, c) in enumerate(items)]
    for n, (hh, c) in enumerate(items):
        r = r_ref[sl(c), dv(hh)]
        o = o_intra[n] + o_inter[n]
        o_ref[sl(c), dv(hh)] = (
            _rms(o) * gon_ref[:, dv(hh)] * (r * jax.nn.sigmoid(r))).astype(o_ref.dtype)
    for hh in range(hps):
        st[hh] = states[hh, nchunks]

    @pl.when(t == pl.num_programs(2) - 1)
    def _():
        for hh in range(hps):
            sout_ref[hh] = states[hh, nchunks].T


def _gla(qg, kg, vg, gate, rg, g_onorm, s0, batch, seq, rows, chunk, hps):
    has_s0 = s0 is not None
    nchunks = rows // chunk
    r3 = lambda a: a.reshape(batch, seq, a.shape[-1])
    tok = lambda w: pl.BlockSpec((None, rows, hps * w), lambda b, h, t: (b, t, h))
    state_spec = pl.BlockSpec((None, hps, GLA_DK, GLA_DV), lambda b, h, t: (b, h, 0, 0))
    in_specs = [tok(GLA_DK), tok(GLA_DK), tok(GLA_DV), tok(GLA_DK), tok(GLA_DV),
                pl.BlockSpec((1, hps * GLA_DV), lambda b, h, t: (0, h))]
    args = [r3(qg), r3(kg), r3(vg), r3(gate), r3(rg), g_onorm]
    if has_s0:
        in_specs.append(state_spec)
        args.append(s0)
    return pl.pallas_call(
        functools.partial(_gla_kernel, chunk=chunk, nchunks=nchunks, hps=hps, has_s0=has_s0),
        grid=(batch, GLA_HEADS // hps, seq // rows),
        in_specs=in_specs,
        out_specs=[tok(GLA_DV), state_spec],
        out_shape=[jax.ShapeDtypeStruct((batch, seq, GLA_WIDTH), BF16),
                   jax.ShapeDtypeStruct((batch, GLA_HEADS, GLA_DK, GLA_DV), F32)],
        scratch_shapes=[pltpu.VMEM((hps, GLA_DV, GLA_DK), F32)],
        compiler_params=pltpu.CompilerParams(
            dimension_semantics=("arbitrary", "arbitrary", "arbitrary"), vmem_limit_bytes=VMEM_LIMIT),
        name="gla",
    )(*args)


def _cumsum_kernel(x_ref, o_ref, *, nblk):
    x = x_ref[...]
    length = nblk * LANES
    r = lax.broadcasted_iota(jnp.int32, (LANES, LANES), 0)
    c = lax.broadcasted_iota(jnp.int32, (LANES, LANES), 1)
    triu = (r <= c).astype(F32)
    li = lax.broadcasted_iota(jnp.int32, (length, LANES), 0)
    ji = lax.broadcasted_iota(jnp.int32, (length, LANES), 1)
    before = (li < ji * LANES).astype(F32)
    offs = jnp.dot(x, before, precision=HIGHEST, preferred_element_type=F32)
    for j in range(nblk):
        blk = x[:, j * LANES:(j + 1) * LANES]
        o_ref[:, j * LANES:(j + 1) * LANES] = (
            jnp.dot(blk, triu, precision=HIGHEST, preferred_element_type=F32) + offs[:, j:j + 1])


def _cumsum(xt):
    spec = pl.BlockSpec(xt.shape, lambda i: (0, 0))
    return pl.pallas_call(
        functools.partial(_cumsum_kernel, nblk=xt.shape[1] // LANES),
        grid=(1,),
        in_specs=[spec],
        out_specs=spec,
        out_shape=jax.ShapeDtypeStruct(xt.shape, F32),
        compiler_params=pltpu.CompilerParams(dimension_semantics=("arbitrary",)),
        name="cumsum",
    )(xt)


def _pick_row(block, h):
    sel = lax.broadcasted_iota(jnp.int32, (FOX_HEADS, 1), 0) == h
    return jnp.sum(jnp.where(sel, block, 0.0), axis=0, keepdims=True)


def _pick_col(block, h):
    sel = lax.broadcasted_iota(jnp.int32, (1, FOX_HEADS), 1) == h
    return jnp.sum(jnp.where(sel, block, 0.0), axis=1, keepdims=True)


def _fox_prompt_kernel(q_ref, k_ref, v_ref, crow_ref, ccol_ref, o_ref, s_scr, *, tq, tk, nq):
    h = pl.program_id(1)
    crow = _pick_row(crow_ref[...], h) * LOG2E
    ccol = _pick_col(ccol_ref[...], h) * LOG2E
    rows = lambda qi: slice(qi * tq, (qi + 1) * tq)

    def pieces(qi):
        end = (qi + 1) * tq
        return [(c0, min(c0 + tk, end)) for c0 in range(0, end, tk)]

    def pass1(qi):
        q = q_ref[rows(qi), :]
        cq = ccol[rows(qi), :]
        mx = None
        ps = pieces(qi)
        for n, (c0, c1) in enumerate(ps):
            s = lax.dot_general(q, k_ref[c0:c1, :], NT_DIMS, preferred_element_type=F32)
            s = s + (cq - crow[:, c0:c1])
            if n == len(ps) - 1:
                qpos = lax.broadcasted_iota(jnp.int32, s.shape, 0) + qi * tq
                kpos = lax.broadcasted_iota(jnp.int32, s.shape, 1) + c0
                s = jnp.where(kpos <= qpos, s, NEG)
            s_scr[qi, n, :, :c1 - c0] = s
            part = _lane_fold(s, jnp.maximum)
            mx = part if mx is None else jnp.maximum(mx, part)
        return jnp.max(mx, axis=1, keepdims=True)

    def pass2(qi, m):
        lsum = acc = None
        for n, (c0, c1) in enumerate(pieces(qi)):
            p = jnp.exp2(s_scr[qi, n, :, :c1 - c0] - m)
            pv = jnp.dot(p.astype(BF16), v_ref[c0:c1, :], preferred_element_type=F32)
            part = _lane_fold(p, jnp.add)
            lsum = part if lsum is None else lsum + part
            acc = pv if acc is None else acc + pv
        o_ref[rows(qi), :] = (acc / jnp.sum(lsum, axis=1, keepdims=True)).astype(o_ref.dtype)

    m_prev = pass1(0)
    for qi in range(1, nq):
        m_next = pass1(qi)
        pass2(qi - 1, m_prev)
        m_prev = m_next
    pass2(nq - 1, m_prev)


def _fox_prompt(qh, kh, vh, c_row, c_col, batch, seq, tq, tk):
    nq = seq // tq
    whole = pl.BlockSpec((None, None, seq, FOX_HD), lambda b, h: (b, h, 0, 0))
    return pl.pallas_call(
        functools.partial(_fox_prompt_kernel, tq=tq, tk=tk, nq=nq),
        grid=(batch, FOX_HEADS),
        in_specs=[
            whole, whole, whole,
            pl.BlockSpec((FOX_HEADS, seq), lambda b, h: (b, 0)),
            pl.BlockSpec((None, seq, FOX_HEADS), lambda b, h: (b, 0, 0)),
        ],
        out_specs=pl.BlockSpec((None, seq, FOX_HD), lambda b, h: (b, 0, h)),
        out_shape=jax.ShapeDtypeStruct((batch, seq, FOX_WIDTH), BF16),
        scratch_shapes=[pltpu.VMEM((nq, seq // tk, tq, tk), F32)],
        compiler_params=pltpu.CompilerParams(
            dimension_semantics=("arbitrary", "arbitrary"), vmem_limit_bytes=VMEM_LIMIT),
        name="fox_prompt",
    )(qh, kh, vh, c_row, c_col)


FOX_TKC = 512


def _fox_sample_kernel(q_ref, kc_ref, vc_ref, kn_ref, vn_ref, cc_ref, cn_ref, cq_ref, o_ref,
                       m_scr, l_scr, acc_scr, *, new):
    c = pl.program_id(1)

    @pl.when(c == 0)
    def _():
        m_scr[...] = jnp.full_like(m_scr, NEG)
        l_scr[...] = jnp.zeros_like(l_scr)
        acc_scr[...] = jnp.zeros_like(acc_scr)

    q = q_ref[...]
    cq = cq_ref[...] * LOG2E

    def logits(k, ck):
        s = jnp.einsum('hqd,hkd->hqk', q, k, preferred_element_type=F32)
        return s + (cq - ck * LOG2E)

    def update(s, v):
        m_old = m_scr[...]
        m_new = jnp.maximum(m_old, jnp.max(s, axis=2, keepdims=True))
        alpha = jnp.exp2(m_old - m_new)
        p = jnp.exp2(s - m_new)
        l_scr[...] = alpha * l_scr[...] + jnp.sum(p, axis=2, keepdims=True)
        pv = jnp.einsum('hqk,hkd->hqd', p.astype(BF16), v, preferred_element_type=F32)
        acc_scr[...] = alpha * acc_scr[...] + pv
        m_scr[...] = m_new

    def head_rows(ref):
        return jnp.stack([ref[pl.ds(h, FOX_TKC, stride=FOX_HEADS), :]
                          for h in range(FOX_HEADS)]).astype(BF16)

    update(logits(head_rows(kc_ref), cc_ref[...]), head_rows(vc_ref))

    @pl.when(c == pl.num_programs(1) - 1)
    def _():
        row = lax.broadcasted_iota(jnp.int32, (new, new), 0)
        col = lax.broadcasted_iota(jnp.int32, (new, new), 1)
        s = logits(kn_ref[...], cn_ref[:, :, :new])
        update(jnp.where((col <= row)[None], s, NEG), vn_ref[...])
        out = acc_scr[...] / l_scr[...]
        for h in range(FOX_HEADS):
            o_ref[:, h * FOX_HD:(h + 1) * FOX_HD] = out[h].astype(o_ref.dtype)


def _fox_sample(qh, k_hist, v_hist, layer, kh, vh, c_row, c_q, batch, new):
    past = k_hist.shape[2] // FOX_HEADS
    nkc = past // FOX_TKC
    heads_new = pl.BlockSpec((None, FOX_HEADS, new, FOX_HD), lambda b, c: (b, 0, 0, 0))
    hist = pl.BlockSpec((None, None, FOX_TKC * FOX_HEADS, FOX_HD), lambda b, c: (layer, b, c, 0))
    return pl.pallas_call(
        functools.partial(_fox_sample_kernel, new=new),
        grid=(batch, nkc),
        in_specs=[heads_new, hist, hist, heads_new, heads_new,
                  pl.BlockSpec((FOX_HEADS, 1, FOX_TKC), lambda b, c: (b, 0, c)),
                  pl.BlockSpec((FOX_HEADS, 1, FOX_TKC), lambda b, c: (b, 0, nkc)),
                  pl.BlockSpec((None, FOX_HEADS, new, 1), lambda b, c: (b, 0, 0, 0))],
        out_specs=pl.BlockSpec((None, new, FOX_WIDTH), lambda b, c: (b, 0, 0)),
        out_shape=jax.ShapeDtypeStruct((batch, new, FOX_WIDTH), BF16),
        scratch_shapes=[pltpu.VMEM((FOX_HEADS, new, 1), F32),
                        pltpu.VMEM((FOX_HEADS, new, 1), F32),
                        pltpu.VMEM((FOX_HEADS, new, FOX_HD), F32)],
        compiler_params=pltpu.CompilerParams(
            dimension_semantics=("arbitrary", "arbitrary"), vmem_limit_bytes=VMEM_LIMIT),
        name="fox_sample",
    )(qh, k_hist, v_hist, kh, vh, c_row[:, None, :], c_row[:, None, :], c_q)


OUT_PARTS = 4


def _out_proj_kernel(og_ref, of_ref, w_ref, x_ref, gpost_ref, gpre_ref, x1_ref, h2_ref):
    rows = og_ref.shape[0] // OUT_PARTS
    part = lambda a: slice(a * rows, (a + 1) * rows)
    ys = []
    for a in range(OUT_PARTS):
        y = jnp.dot(og_ref[part(a), :], w_ref[:GLA_WIDTH, :], preferred_element_type=F32)
        ys.append(y + jnp.dot(of_ref[part(a), :], w_ref[GLA_WIDTH:, :], preferred_element_type=F32))
    for a in range(OUT_PARTS):
        x1 = x_ref[part(a), :] + _rms(ys[a]) * gpost_ref[...]
        x1_ref[part(a), :] = x1
        h2_ref[part(a), :] = (_rms(x1) * gpre_ref[...]).astype(BF16)


def _out_proj(o_gla, o_fox, w_out, x, g_post, g_mlp_pre, tm):
    m = x.shape[0]
    const = lambda i: (0, 0)
    rowblk = lambda w: pl.BlockSpec((tm, w), lambda i: (i, 0))
    return pl.pallas_call(
        _out_proj_kernel,
        grid=(m // tm,),
        in_specs=[rowblk(GLA_WIDTH), rowblk(FOX_WIDTH),
                  pl.BlockSpec((D_MODEL, D_MODEL), const),
                  rowblk(D_MODEL),
                  pl.BlockSpec((1, D_MODEL), const), pl.BlockSpec((1, D_MODEL), const)],
        out_specs=[rowblk(D_MODEL), rowblk(D_MODEL)],
        out_shape=[jax.ShapeDtypeStruct((m, D_MODEL), F32), jax.ShapeDtypeStruct((m, D_MODEL), BF16)],
        compiler_params=pltpu.CompilerParams(
            dimension_semantics=("arbitrary",), vmem_limit_bytes=VMEM_LIMIT),
        name="out_proj",
    )(o_gla.reshape(m, GLA_WIDTH), o_fox.reshape(m, FOX_WIDTH), w_out, x, g_post, g_mlp_pre)


MLP_TF = 1024
MLP_TN = 512


def _mlp_kernel(*refs, has_next):
    if has_next:
        h_ref, wu_ref, wd_ref, x1_ref, g_ref, gnext_ref, o_ref, hn_ref, u_scr = refs
    else:
        h_ref, wu_ref, wd_ref, x1_ref, g_ref, o_ref, u_scr = refs
    j = pl.program_id(1)

    def step(first):
        u = jnp.maximum(jnp.dot(h_ref[...], wu_ref[...], preferred_element_type=F32), 0.0)
        u_scr[...] = (u * u).astype(BF16)
        for n in range(0, D_MODEL, MLP_TN):
            part = jnp.dot(u_scr[...], wd_ref[:, n:n + MLP_TN], preferred_element_type=F32)
            if first:
                o_ref[:, n:n + MLP_TN] = part
            else:
                o_ref[:, n:n + MLP_TN] += part

    pl.when(j == 0)(functools.partial(step, True))
    pl.when(j > 0)(functools.partial(step, False))

    @pl.when(j == pl.num_programs(1) - 1)
    def _():
        x2 = x1_ref[...] + _rms(o_ref[...]) * g_ref[...]
        o_ref[...] = x2
        if has_next:
            hn_ref[...] = (_rms(x2) * gnext_ref[...]).astype(BF16)


def _mlp(h2, w_up, w_down, x1, g_post, g_next, tm):
    m = x1.shape[0]
    has_next = g_next is not None
    rowblk = pl.BlockSpec((tm, D_MODEL), lambda i, j: (i, 0))
    gain = pl.BlockSpec((1, D_MODEL), lambda i, j: (0, 0))
    in_specs = [rowblk,
                pl.BlockSpec((D_MODEL, MLP_TF), lambda i, j: (0, j)),
                pl.BlockSpec((MLP_TF, D_MODEL), lambda i, j: (j, 0)),
                rowblk, gain]
    args = [h2, w_up, w_down, x1, g_post]
    out_specs = [rowblk]
    out_shape = [jax.ShapeDtypeStruct((m, D_MODEL), F32)]
    if has_next:
        in_specs.append(gain)
        args.append(g_next)
        out_specs.append(rowblk)
        out_shape.append(jax.ShapeDtypeStruct((m, D_MODEL), BF16))
    outs = pl.pallas_call(
        functools.partial(_mlp_kernel, has_next=has_next),
        grid=(m // tm, D_FF // MLP_TF),
        in_specs=in_specs,
        out_specs=out_specs,
        out_shape=out_shape,
        scratch_shapes=[pltpu.VMEM((tm, MLP_TF), BF16)],
        compiler_params=pltpu.CompilerParams(
            dimension_semantics=("arbitrary", "arbitrary"), vmem_limit_bytes=VMEM_LIMIT),
        name="mlp",
    )(*args)
    return (outs[0], outs[1]) if has_next else (outs[0], None)


def _layer_weights(l, g_mix_pre, w_in, w_gla_gate_up, b_gla_gate, b_fox_f, g_gla_onorm,
                   g_mix_post, g_mlp_pre, g_mlp_post):
    row = lambda v: v.reshape(1, -1)
    w_t = _cast_w_in(jnp.swapaxes(w_in, 1, 2), l)
    w_small_t = jnp.concatenate([w_t[O_GLR:O_RG], w_t[O_FL:]], axis=0)
    return dict(
        g_pre=row(g_mix_pre[l]),
        w_t=w_t,
        w_small_t=jnp.pad(w_small_t, ((0, LANES - GLA_RANK - FOX_HEADS), (0, 0))),
        w_gu=jnp.pad(w_gla_gate_up[l], ((0, LANES - GLA_RANK), (0, 0))).astype(BF16),
        b_gate=row(b_gla_gate[l]),
        b_f=row(b_fox_f[l]),
        b_ft=b_fox_f[l].reshape(-1, 1),
        g_onorm=row(g_gla_onorm[l]),
        g_post=row(g_mix_post[l]),
        g_mlp_pre=row(g_mlp_pre[l]),
        g_mlp_post=row(g_mlp_post[l]),
    )


def _layer(x, h, wts, g_next, batch, seq, layer, kv_prev, tm_proj, tm, gla_rows, gla_chunk,
           s0=None, cache=None, cast=None):
    outs = _in_proj(h, wts, batch, seq, tm_proj, layer, kv_prev, cast or ())
    qg, kg, vg, rg, gate, logf, logft, qh, kh, vh, kf, vf = outs[:N_OUT]
    if cast:
        wts["w_up"], wts["w_down"], wts["w_out"] = outs[N_OUT:]
    hps = GLA_HEADS if gla_rows == gla_chunk else 1
    o_gla, s_new = _gla(qg, kg, vg, gate, rg, wts["g_onorm"], s0, batch, seq, gla_rows, gla_chunk, hps)
    if cache is None:
        lf = logft.reshape(FOX_HEADS, batch, seq).transpose(1, 0, 2)
        c_row = _cumsum(lf.reshape(batch * FOX_HEADS, seq))
        c_col = c_row.reshape(batch, FOX_HEADS, seq).transpose(0, 2, 1)
        o_fox = _fox_prompt(qh, kh, vh, c_row, c_col, batch, seq, tq=256, tk=512)
    else:
        k_hist, v_hist, logf_cache = cache
        past = logf_cache.shape[1]
        total = past + seq
        seg = past + FOX_TKC
        lf = jnp.concatenate([logf_cache, logf.reshape(batch, seq, FOX_HEADS)], axis=1)
        lf = jnp.pad(lf.transpose(0, 2, 1), ((0, 0), (0, 0), (0, seg - total)))
        c_row = _cumsum(lf.reshape(batch * FOX_HEADS, seg))
        c_q = c_row.reshape(batch, FOX_HEADS, seg)[:, :, past:total][..., None]
        o_fox = _fox_sample(qh, k_hist, v_hist, layer, kh, vh, c_row, c_q, batch, seq)
    x1, h2 = _out_proj(o_gla, o_fox, wts["w_out"], x, wts["g_post"], wts["g_mlp_pre"], tm)
    x2, h_next = _mlp(h2, wts["w_up"], wts["w_down"], x1, wts["g_mlp_post"], g_next, tm)
    return x2, h_next, (kf, vf), logf.reshape(batch, seq, FOX_HEADS), s_new


def kernel(x_prompt, x_sample, cache_fox_k, cache_fox_v, cache_fox_logf, state_gla, g_mix_pre, w_in,
           w_gla_gate_up, b_gla_gate, b_fox_f, g_gla_onorm, w_out, g_mix_post, g_mlp_pre, w_mlp_up,
           w_mlp_down, g_mlp_post):
    pb, pl_, _ = x_prompt.shape
    sb, sl, _ = x_sample.shape
    yp = x_prompt.reshape(pb * pl_, D_MODEL)
    ys = x_sample.reshape(sb * sl, D_MODEL)
    past = cache_fox_k.shape[2]
    k_hist = cache_fox_k.reshape(DEPTH, sb, past * FOX_HEADS, FOX_HD)
    v_hist = cache_fox_v.reshape(DEPTH, sb, past * FOX_HEADS, FOX_HD)
    g0 = g_mix_pre[0].reshape(1, -1)
    hp = _norm(yp, g0, 512)
    hs = _norm(ys, g0, sb * sl)
    kv_p = kv_s = None
    logf_p, logf_s, state_p, state_s = [], [], [], []
    for l in range(DEPTH):
        wts = _layer_weights(l, g_mix_pre, w_in, w_gla_gate_up, b_gla_gate, b_fox_f, g_gla_onorm,
                             g_mix_post, g_mlp_pre, g_mlp_post)
        g_next = g_mix_pre[l + 1].reshape(1, -1) if l + 1 < DEPTH else None
        yp, hp, kv_p, lf, st = _layer(yp, hp, wts, g_next, pb, pl_, l, kv_p, tm_proj=256, tm=512,
                                      gla_rows=1024, gla_chunk=CHUNK,
                                      cast=(w_mlp_up, w_mlp_down, w_out))
        logf_p.append(lf)
        state_p.append(st)
        ys, hs, kv_s, lf, st = _layer(ys, hs, wts, g_next, sb, sl, l, kv_s, tm_proj=sb * sl,
                                      tm=sb * sl, gla_rows=sl, gla_chunk=sl, s0=state_gla[l],
                                      cache=(k_hist, v_hist, cache_fox_logf[l]))
        logf_s.append(lf)
        state_s.append(st)
    kv5 = lambda a, b, s: a.reshape(DEPTH, b, s, FOX_HEADS, FOX_HD)
    return (yp.reshape(x_prompt.shape), ys.reshape(x_sample.shape),
            kv5(kv_p[0], pb, pl_), kv5(kv_p[1], pb, pl_), jnp.stack(logf_p), jnp.stack(state_p),
            kv5(kv_s[0], sb, sl), kv5(kv_s[1], sb, sl), jnp.stack(logf_s), jnp.stack(state_s))
```

```python
import functools

import jax
import jax.numpy as jnp
from jax import lax
from jax.experimental import pallas as pl
from jax.experimental.pallas import tpu as pltpu

F32 = jnp.float32
BF16 = jnp.bfloat16
HIGHEST = lax.Precision.HIGHEST

D_MODEL = 2048
DEPTH = 2
CHUNK = 64
GLA_HEADS = 4
GLA_DK = 128
GLA_DV = 256
GLA_QK = GLA_HEADS * GLA_DK
GLA_WIDTH = GLA_HEADS * GLA_DV
GLA_RANK = 16
GLA_TAU = 16.0
FOX_HEADS = 8
FOX_HD = 128
FOX_WIDTH = FOX_HEADS * FOX_HD
D_FF = 4 * D_MODEL
EPS = 1e-6
LOG2E = 1.4426950408889634
NEG = -1e30

O_GLR = 2 * GLA_QK + GLA_WIDTH
O_RG = O_GLR + GLA_RANK
O_FL = O_RG + GLA_WIDTH + 3 * FOX_WIDTH
PROJ_WIDTH = O_FL + FOX_HEADS
MAIN_WIDTH = PROJ_WIDTH - GLA_RANK - FOX_HEADS
LANES = 128
SMALL_FL = GLA_RANK

VMEM_LIMIT = 52 * 1024 * 1024
NT_DIMS = (((1,), (1,)), ((), ()))
TN_DIMS = (((0,), (0,)), ((), ()))


def _log_sigmoid(x):
    return jnp.minimum(x, 0.0) - jnp.log1p(jnp.exp(-jnp.abs(x)))


def _rms(x):
    return x * lax.rsqrt(jnp.mean(x * x, axis=-1, keepdims=True) + EPS)


def _lane_fold(x, op):
    parts = [x[:, i * LANES:(i + 1) * LANES] for i in range(x.shape[1] // LANES)]
    while len(parts) > 1:
        parts = [op(parts[i], parts[i + 1]) for i in range(0, len(parts), 2)]
    return parts[0]


def _norm_kernel(x_ref, g_ref, h_ref):
    h_ref[...] = (_rms(x_ref[...]) * g_ref[...]).astype(BF16)


def _norm(x, g, tm):
    m = x.shape[0]
    return pl.pallas_call(
        _norm_kernel,
        grid=(m // tm,),
        in_specs=[pl.BlockSpec((tm, D_MODEL), lambda i: (i, 0)),
                  pl.BlockSpec((1, D_MODEL), lambda i: (0, 0))],
        out_specs=pl.BlockSpec((tm, D_MODEL), lambda i: (i, 0)),
        out_shape=jax.ShapeDtypeStruct((m, D_MODEL), BF16),
        compiler_params=pltpu.CompilerParams(
            dimension_semantics=("arbitrary",), vmem_limit_bytes=VMEM_LIMIT),
        name="norm",
    )(x, g)


CAST_ROWS = 512


def _cast_kernel(w_ref, o_ref):
    o_ref[...] = w_ref[...].astype(BF16)


def _cast_w_in(w_t, layer):
    return pl.pallas_call(
        _cast_kernel,
        grid=(pl.cdiv(PROJ_WIDTH, CAST_ROWS),),
        in_specs=[pl.BlockSpec((None, CAST_ROWS, D_MODEL), lambda i: (layer, i, 0))],
        out_specs=pl.BlockSpec((CAST_ROWS, D_MODEL), lambda i: (i, 0)),
        out_shape=jax.ShapeDtypeStruct((PROJ_WIDTH, D_MODEL), BF16),
        compiler_params=pltpu.CompilerParams(
            dimension_semantics=("arbitrary",), vmem_limit_bytes=VMEM_LIMIT),
        name="cast_w_in",
    )(w_t)


IN_TN = 512
HEADS_PER_TN = IN_TN // FOX_HD
GLA_BLOCKS = O_GLR // IN_TN
N_IN = 7
N_OUT = 12
IN_PROJ_VMEM_LIMIT = 58 * 1024 * 1024


def _in_proj_kernel(*refs, nb, lt, n_alias, n_cast):
    h_ref, wt_ref, ws_ref, wgu_ref, bg_ref, bf_ref, bft_ref = refs[:N_IN]
    cast_in = refs[N_IN:N_IN + n_cast]
    outs = refs[N_IN + n_cast + n_alias:]
    (qg_ref, kg_ref, vg_ref, rg_ref, gate_ref, logf_ref, logft_ref,
     qh_ref, kh_ref, vh_ref, kf_ref, vf_ref) = outs[:N_OUT]
    tm = nb * lt
    h = h_ref[...]

    for src, dst in zip(cast_in, outs[N_OUT:]):
        dst[...] = src[...].astype(BF16)

    zs = lax.dot_general(h, ws_ref[...], NT_DIMS, preferred_element_type=F32)
    zst = lax.dot_general(ws_ref[...], h, NT_DIMS, preferred_element_type=F32)
    gpre = jnp.dot(zs.astype(BF16), wgu_ref[...], preferred_element_type=F32) + bg_ref[...]
    gate_ref[...] = _log_sigmoid(gpre) / GLA_TAU
    logf_ref[...] = _log_sigmoid(zs[:, SMALL_FL:SMALL_FL + FOX_HEADS] + bf_ref[...])
    logft_ref[...] = _log_sigmoid(zst[SMALL_FL:SMALL_FL + FOX_HEADS, :] + bft_ref[...])

    def mm(cb):
        r0 = cb * IN_TN + (GLA_RANK if cb >= GLA_BLOCKS else 0)
        return lax.dot_general(h, wt_ref[r0:r0 + IN_TN, :], NT_DIMS, preferred_element_type=F32)

    def heads(ref, val, half):
        for hh in range(HEADS_PER_TN):
            ref[:, half * HEADS_PER_TN + hh] = (
                val[:, hh * FOX_HD:(hh + 1) * FOX_HD].astype(BF16).reshape(nb, lt, FOX_HD))

    def rows(ref, val, half):
        for hh in range(HEADS_PER_TN):
            ref[pl.ds(half * HEADS_PER_TN + hh, tm, stride=FOX_HEADS), :] = (
                val[:, hh * FOX_HD:(hh + 1) * FOX_HD])

    qg_ref[...] = mm(0)
    kg_ref[...] = mm(1)
    for half in range(2):
        cols = slice(half * IN_TN, (half + 1) * IN_TN)
        vg_ref[:, cols] = mm(2 + half).astype(BF16)
        rg_ref[:, cols] = mm(4 + half)
        heads(qh_ref, mm(6 + half) * (FOX_HD ** -0.5 * LOG2E), half)
        val = mm(8 + half)
        heads(kh_ref, val, half)
        rows(kf_ref, val, half)
        val = mm(10 + half)
        heads(vh_ref, val, half)
        rows(vf_ref, val, half)


def _in_proj(h, wts, batch, seq, tm, layer, kv_prev, cast=()):
    m = h.shape[0]
    steps = m // tm
    lt = min(tm, seq)
    nb = tm // lt
    nt = seq // lt
    const = lambda i: (0, 0)
    rowblk = lambda w: pl.BlockSpec((tm, w), lambda i: (i, 0))
    head_shape = jax.ShapeDtypeStruct((batch, FOX_HEADS, seq, FOX_HD), BF16)
    head_spec = pl.BlockSpec((nb, FOX_HEADS, lt, FOX_HD), lambda i: (i // nt, 0, i % nt, 0))
    kv_shape = jax.ShapeDtypeStruct((DEPTH, m * FOX_HEADS, FOX_HD), F32)
    kv_spec = pl.BlockSpec((None, tm * FOX_HEADS, FOX_HD), lambda i: (layer, i, 0))
    out_shape = [
        jax.ShapeDtypeStruct((m, GLA_QK), F32), jax.ShapeDtypeStruct((m, GLA_QK), F32),
        jax.ShapeDtypeStruct((m, GLA_WIDTH), BF16), jax.ShapeDtypeStruct((m, GLA_WIDTH), F32),
        jax.ShapeDtypeStruct((m, GLA_QK), F32), jax.ShapeDtypeStruct((m, FOX_HEADS), F32),
        jax.ShapeDtypeStruct((FOX_HEADS, m), F32),
        head_shape, head_shape, head_shape, kv_shape, kv_shape,
    ]
    out_specs = [
        rowblk(GLA_QK), rowblk(GLA_QK), rowblk(GLA_WIDTH), rowblk(GLA_WIDTH),
        rowblk(GLA_QK), rowblk(FOX_HEADS),
        pl.BlockSpec((FOX_HEADS, tm), lambda i: (0, i)),
        head_spec, head_spec, head_spec, kv_spec, kv_spec,
    ]
    assert len(out_specs) == N_OUT
    in_specs = [
        rowblk(D_MODEL),
        pl.BlockSpec((PROJ_WIDTH, D_MODEL), const),
        pl.BlockSpec((LANES, D_MODEL), const),
        pl.BlockSpec((LANES, GLA_QK), const),
        pl.BlockSpec((1, GLA_QK), const),
        pl.BlockSpec((1, FOX_HEADS), const),
        pl.BlockSpec((FOX_HEADS, 1), const),
    ]
    args = [h, wts["w_t"], wts["w_small_t"], wts["w_gu"], wts["b_gate"], wts["b_f"], wts["b_ft"]]
    assert len(args) == N_IN
    for w in cast:
        _, r, c = w.shape
        in_specs.append(pl.BlockSpec((None, r // steps, c), lambda i: (layer, i, 0)))
        out_specs.append(pl.BlockSpec((r // steps, c), lambda i: (i, 0)))
        out_shape.append(jax.ShapeDtypeStruct((r, c), BF16))
        args.append(w)
    aliases = {}
    if kv_prev is not None:
        in_specs += [pl.BlockSpec(memory_space=pl.ANY)] * 2
        aliases = {len(args): N_OUT - 2, len(args) + 1: N_OUT - 1}
        args += list(kv_prev)
    return pl.pallas_call(
        functools.partial(_in_proj_kernel, nb=nb, lt=lt, n_alias=len(aliases), n_cast=len(cast)),
        grid=(steps,),
        in_specs=in_specs,
        out_specs=out_specs,
        out_shape=out_shape,
        input_output_aliases=aliases,
        compiler_params=pltpu.CompilerParams(
            dimension_semantics=("arbitrary",), vmem_limit_bytes=IN_PROJ_VMEM_LIMIT),
        name="in_proj",
    )(*args)


def _gla_kernel(*refs, chunk, nchunks, hps, has_s0):
    if has_s0:
        q_ref, k_ref, v_ref, g_ref, r_ref, gon_ref, s0_ref, o_ref, sout_ref, st = refs
    else:
        q_ref, k_ref, v_ref, g_ref, r_ref, gon_ref, o_ref, sout_ref, st = refs
    t = pl.program_id(2)

    @pl.when(t == 0)
    def _():
        for hh in range(hps):
            st[hh] = s0_ref[hh].T if has_s0 else jnp.zeros((GLA_DV, GLA_DK), F32)

    row = lax.broadcasted_iota(jnp.int32, (chunk, chunk), 0)
    col = lax.broadcasted_iota(jnp.int32, (chunk, chunk), 1)
    causal = col <= row
    tri = causal.astype(BF16)
    qscale = GLA_DK ** -0.5
    items = [(hh, c) for hh in range(hps) for c in range(nchunks)]
    sl = lambda c: pl.ds(c * chunk, chunk)
    dk = lambda hh: slice(hh * GLA_DK, (hh + 1) * GLA_DK)
    dv = lambda hh: slice(hh * GLA_DV, (hh + 1) * GLA_DV)

    def cumsum(g):
        b = None
        for _ in range(3):
            piece = g.astype(BF16)
            part = jnp.dot(tri, piece, preferred_element_type=F32)
            b = part if b is None else b + part
            g = g - piece.astype(F32)
        return b

    bs = [cumsum(g_ref[sl(c), dk(hh)]) for hh, c in items]
    b_lasts = [b[chunk - 1:chunk, :] for b in bs]
    q_is = [(q_ref[sl(c), dk(hh)] * qscale * jnp.exp(bs[n])).astype(BF16)
            for n, (hh, c) in enumerate(items)]
    k_is = [(k_ref[sl(c), dk(hh)] * jnp.exp(-bs[n])).astype(BF16) for n, (hh, c) in enumerate(items)]
    k_es = [(k_ref[sl(c), dk(hh)] * jnp.exp(b_lasts[n] - bs[n])).astype(BF16)
            for n, (hh, c) in enumerate(items)]
    atts = [lax.dot_general(q_is[n], k_is[n], NT_DIMS, preferred_element_type=F32)
            for n in range(len(items))]
    kv_ts = [lax.dot_general(v_ref[sl(c), dv(hh)], k_es[n], TN_DIMS, preferred_element_type=F32)
             for n, (hh, c) in enumerate(items)]
    atts = [jnp.where(causal, a, 0.0).astype(BF16) for a in atts]
    o_intra = [jnp.dot(atts[n], v_ref[sl(c), dv(hh)], preferred_element_type=F32)
               for n, (hh, c) in enumerate(items)]
    states = {}
    for n, (hh, c) in enumerate(items):
        prev = st[hh] if c == 0 else states[hh, c]
        states[hh, c] = prev
        states[hh, c + 1] = prev * jnp.exp(b_lasts[n]) + kv_ts[n]
    o_inter = [lax.dot_general(q_is[n], states[hh, c].astype(BF16), NT_DIMS,
                               preferred_element_type=F32) for n, (hh, c) in enumerate(items)]
    for n, (hh, c) in enumerate(items):
        r = r_ref[sl(c), dv(hh)]
        o = o_intra[n] + o_inter[n]
        o_ref[sl(c), dv(hh)] = (
            _rms(o) * gon_ref[:, dv(hh)] * (r * jax.nn.sigmoid(r))).astype(o_ref.dtype)
    for hh in range(hps):
        st[hh] = states[hh, nchunks]

    @pl.when(t == pl.num_programs(2) - 1)
    def _():
        for hh in range(hps):
            sout_ref[hh] = states[hh, nchunks].T


def _gla(qg, kg, vg, gate, rg, g_onorm, s0, batch, seq, rows, chunk, hps):
    has_s0 = s0 is not None
    nchunks = rows // chunk
    r3 = lambda a: a.reshape(batch, seq, a.shape[-1])
    tok = lambda w: pl.BlockSpec((None, rows, hps * w), lambda b, h, t: (b, t, h))
    state_spec = pl.BlockSpec((None, hps, GLA_DK, GLA_DV), lambda b, h, t: (b, h, 0, 0))
    in_specs = [tok(GLA_DK), tok(GLA_DK), tok(GLA_DV), tok(GLA_DK), tok(GLA_DV),
                pl.BlockSpec((1, hps * GLA_DV), lambda b, h, t: (0, h))]
    args = [r3(qg), r3(kg), r3(vg), r3(gate), r3(rg), g_onorm]
    if has_s0:
        in_specs.append(state_spec)
        args.append(s0)
    return pl.pallas_call(
        functools.partial(_gla_kernel, chunk=chunk, nchunks=nchunks, hps=hps, has_s0=has_s0),
        grid=(batch, GLA_HEADS // hps, seq // rows),
        in_specs=in_specs,
        out_specs=[tok(GLA_DV), state_spec],
        out_shape=[jax.ShapeDtypeStruct((batch, seq, GLA_WIDTH), BF16),
                   jax.ShapeDtypeStruct((batch, GLA_HEADS, GLA_DK, GLA_DV), F32)],
        scratch_shapes=[pltpu.VMEM((hps, GLA_DV, GLA_DK), F32)],
        compiler_params=pltpu.CompilerParams(
            dimension_semantics=("arbitrary", "arbitrary", "arbitrary"), vmem_limit_bytes=VMEM_LIMIT),
        name="gla",
    )(*args)


def _cumsum_kernel(x_ref, o_ref, *, nblk):
    x = x_ref[...]
    length = nblk * LANES
    r = lax.broadcasted_iota(jnp.int32, (LANES, LANES), 0)
    c = lax.broadcasted_iota(jnp.int32, (LANES, LANES), 1)
    triu = (r <= c).astype(F32)
    li = lax.broadcasted_iota(jnp.int32, (length, LANES), 0)
    ji = lax.broadcasted_iota(jnp.int32, (length, LANES), 1)
    before = (li < ji * LANES).astype(F32)
    offs = jnp.dot(x, before, precision=HIGHEST, preferred_element_type=F32)
    for j in range(nblk):
        blk = x[:, j * LANES:(j + 1) * LANES]
        o_ref[:, j * LANES:(j + 1) * LANES] = (
            jnp.dot(blk, triu, precision=HIGHEST, preferred_element_type=F32) + offs[:, j:j + 1])


def _cumsum(xt):
    spec = pl.BlockSpec(xt.shape, lambda i: (0, 0))
    return pl.pallas_call(
        functools.partial(_cumsum_kernel, nblk=xt.shape[1] // LANES),
        grid=(1,),
        in_specs=[spec],
        out_specs=spec,
        out_shape=jax.ShapeDtypeStruct(xt.shape, F32),
        compiler_params=pltpu.CompilerParams(dimension_semantics=("arbitrary",)),
        name="cumsum",
    )(xt)


def _pick_row(block, h):
    sel = lax.broadcasted_iota(jnp.int32, (FOX_HEADS, 1), 0) == h
    return jnp.sum(jnp.where(sel, block, 0.0), axis=0, keepdims=True)


def _pick_col(block, h):
    sel = lax.broadcasted_iota(jnp.int32, (1, FOX_HEADS), 1) == h
    return jnp.sum(jnp.where(sel, block, 0.0), axis=1, keepdims=True)


def _fox_prompt_kernel(q_ref, k_ref, v_ref, crow_ref, ccol_ref, o_ref, s_scr, *, tq, tk, nq):
    h = pl.program_id(1)
    crow = _pick_row(crow_ref[...], h) * LOG2E
    ccol = _pick_col(ccol_ref[...], h) * LOG2E
    rows = lambda qi: slice(qi * tq, (qi + 1) * tq)

    def pieces(qi):
        end = (qi + 1) * tq
        return [(c0, min(c0 + tk, end)) for c0 in range(0, end, tk)]

    def pass1(qi):
        q = q_ref[rows(qi), :]
        cq = ccol[rows(qi), :]
        mx = None
        ps = pieces(qi)
        for n, (c0, c1) in enumerate(ps):
            s = lax.dot_general(q, k_ref[c0:c1, :], NT_DIMS, preferred_element_type=F32)
            s = s + (cq - crow[:, c0:c1])
            if n == len(ps) - 1:
                qpos = lax.broadcasted_iota(jnp.int32, s.shape, 0) + qi * tq
                kpos = lax.broadcasted_iota(jnp.int32, s.shape, 1) + c0
                s = jnp.where(kpos <= qpos, s, NEG)
            s_scr[qi, n, :, :c1 - c0] = s
            part = _lane_fold(s, jnp.maximum)
            mx = part if mx is None else jnp.maximum(mx, part)
        return jnp.max(mx, axis=1, keepdims=True)

    def pass2(qi, m):
        lsum = acc = None
        for n, (c0, c1) in enumerate(pieces(qi)):
            p = jnp.exp2(s_scr[qi, n, :, :c1 - c0] - m)
            pv = jnp.dot(p.astype(BF16), v_ref[c0:c1, :], preferred_element_type=F32)
            part = _lane_fold(p, jnp.add)
            lsum = part if lsum is None else lsum + part
            acc = pv if acc is None else acc + pv
        o_ref[rows(qi), :] = (acc / jnp.sum(lsum, axis=1, keepdims=True)).astype(o_ref.dtype)

    m_prev = pass1(0)
    for qi in range(1, nq):
        m_next = pass1(qi)
        pass2(qi - 1, m_prev)
        m_prev = m_next
    pass2(nq - 1, m_prev)


def _fox_prompt(qh, kh, vh, c_row, c_col, batch, seq, tq, tk):
    nq = seq // tq
    whole = pl.BlockSpec((None, None, seq, FOX_HD), lambda b, h: (b, h, 0, 0))
    return pl.pallas_call(
        functools.partial(_fox_prompt_kernel, tq=tq, tk=tk, nq=nq),
        grid=(batch, FOX_HEADS),
        in_specs=[
            whole, whole, whole,
            pl.BlockSpec((FOX_HEADS, seq), lambda b, h: (b, 0)),
            pl.BlockSpec((None, seq, FOX_HEADS), lambda b, h: (b, 0, 0)),
        ],
        out_specs=pl.BlockSpec((None, seq, FOX_HD), lambda b, h: (b, 0, h)),
        out_shape=jax.ShapeDtypeStruct((batch, seq, FOX_WIDTH), BF16),
        scratch_shapes=[pltpu.VMEM((nq, seq // tk, tq, tk), F32)],
        compiler_params=pltpu.CompilerParams(
            dimension_semantics=("arbitrary", "arbitrary"), vmem_limit_bytes=VMEM_LIMIT),
        name="fox_prompt",
    )(qh, kh, vh, c_row, c_col)


FOX_TKC = 512


def _fox_sample_kernel(q_ref, kc_ref, vc_ref, kn_ref, vn_ref, cc_ref, cn_ref, cq_ref, o_ref,
                       m_scr, l_scr, acc_scr, *, new):
    c = pl.program_id(1)

    @pl.when(c == 0)
    def _():
        m_scr[...] = jnp.full_like(m_scr, NEG)
        l_scr[...] = jnp.zeros_like(l_scr)
        acc_scr[...] = jnp.zeros_like(acc_scr)

    q = q_ref[...]
    cq = cq_ref[...] * LOG2E

    def logits(k, ck):
        s = jnp.einsum('hqd,hkd->hqk', q, k, preferred_element_type=F32)
        return s + (cq - ck * LOG2E)

    def update(s, v):
        m_old = m_scr[...]
        m_new = jnp.maximum(m_old, jnp.max(s, axis=2, keepdims=True))
        alpha = jnp.exp2(m_old - m_new)
        p = jnp.exp2(s - m_new)
        l_scr[...] = alpha * l_scr[...] + jnp.sum(p, axis=2, keepdims=True)
        pv = jnp.einsum('hqk,hkd->hqd', p.astype(BF16), v, preferred_element_type=F32)
        acc_scr[...] = alpha * acc_scr[...] + pv
        m_scr[...] = m_new

    def head_rows(ref):
        return jnp.stack([ref[pl.ds(h, FOX_TKC, stride=FOX_HEADS), :]
                          for h in range(FOX_HEADS)]).astype(BF16)

    update(logits(head_rows(kc_ref), cc_ref[...]), head_rows(vc_ref))

    @pl.when(c == pl.num_programs(1) - 1)
    def _():
        row = lax.broadcasted_iota(jnp.int32, (new, new), 0)
        col = lax.broadcasted_iota(jnp.int32, (new, new), 1)
        s = logits(kn_ref[...], cn_ref[:, :, :new])
        update(jnp.where((col <= row)[None], s, NEG), vn_ref[...])
        out = acc_scr[...] / l_scr[...]
        for h in range(FOX_HEADS):
            o_ref[:, h * FOX_HD:(h + 1) * FOX_HD] = out[h].astype(o_ref.dtype)


def _fox_sample(qh, k_hist, v_hist, layer, kh, vh, c_row, c_q, batch, new):
    past = k_hist.shape[2] // FOX_HEADS
    nkc = past // FOX_TKC
    heads_new = pl.BlockSpec((None, FOX_HEADS, new, FOX_HD), lambda b, c: (b, 0, 0, 0))
    hist = pl.BlockSpec((None, None, FOX_TKC * FOX_HEADS, FOX_HD), lambda b, c: (layer, b, c, 0))
    return pl.pallas_call(
        functools.partial(_fox_sample_kernel, new=new),
        grid=(batch, nkc),
        in_specs=[heads_new, hist, hist, heads_new, heads_new,
                  pl.BlockSpec((FOX_HEADS, 1, FOX_TKC), lambda b, c: (b, 0, c)),
                  pl.BlockSpec((FOX_HEADS, 1, FOX_TKC), lambda b, c: (b, 0, nkc)),
                  pl.BlockSpec((None, FOX_HEADS, new, 1), lambda b, c: (b, 0, 0, 0))],
        out_specs=pl.BlockSpec((None, new, FOX_WIDTH), lambda b, c: (b, 0, 0)),
        out_shape=jax.ShapeDtypeStruct((batch, new, FOX_WIDTH), BF16),
        scratch_shapes=[pltpu.VMEM((FOX_HEADS, new, 1), F32),
                        pltpu.VMEM((FOX_HEADS, new, 1), F32),
                        pltpu.VMEM((FOX_HEADS, new, FOX_HD), F32)],
        compiler_params=pltpu.CompilerParams(
            dimension_semantics=("arbitrary", "arbitrary"), vmem_limit_bytes=VMEM_LIMIT),
        name="fox_sample",
    )(qh, k_hist, v_hist, kh, vh, c_row[:, None, :], c_row[:, None, :], c_q)


OUT_PARTS = 4


def _out_proj_kernel(og_ref, of_ref, w_ref, x_ref, gpost_ref, gpre_ref, x1_ref, h2_ref):
    rows = og_ref.shape[0] // OUT_PARTS
    part = lambda a: slice(a * rows, (a + 1) * rows)
    ys = []
    for a in range(OUT_PARTS):
        y = jnp.dot(og_ref[part(a), :], w_ref[:GLA_WIDTH, :], preferred_element_type=F32)
        ys.append(y + jnp.dot(of_ref[part(a), :], w_ref[GLA_WIDTH:, :], preferred_element_type=F32))
    for a in range(OUT_PARTS):
        x1 = x_ref[part(a), :] + _rms(ys[a]) * gpost_ref[...]
        x1_ref[part(a), :] = x1
        h2_ref[part(a), :] = (_rms(x1) * gpre_ref[...]).astype(BF16)


def _out_proj(o_gla, o_fox, w_out, x, g_post, g_mlp_pre, tm):
    m = x.shape[0]
    const = lambda i: (0, 0)
    rowblk = lambda w: pl.BlockSpec((tm, w), lambda i: (i, 0))
    return pl.pallas_call(
        _out_proj_kernel,
        grid=(m // tm,),
        in_specs=[rowblk(GLA_WIDTH), rowblk(FOX_WIDTH),
                  pl.BlockSpec((D_MODEL, D_MODEL), const),
                  rowblk(D_MODEL),
                  pl.BlockSpec((1, D_MODEL), const), pl.BlockSpec((1, D_MODEL), const)],
        out_specs=[rowblk(D_MODEL), rowblk(D_MODEL)],
        out_shape=[jax.ShapeDtypeStruct((m, D_MODEL), F32), jax.ShapeDtypeStruct((m, D_MODEL), BF16)],
        compiler_params=pltpu.CompilerParams(
            dimension_semantics=("arbitrary",), vmem_limit_bytes=VMEM_LIMIT),
        name="out_proj",
    )(o_gla.reshape(m, GLA_WIDTH), o_fox.reshape(m, FOX_WIDTH), w_out, x, g_post, g_mlp_pre)


MLP_TF = 1024
MLP_TN = 512


def _mlp_kernel(*refs, has_next):
    if has_next:
        h_ref, wu_ref, wd_ref, x1_ref, g_ref, gnext_ref, o_ref, hn_ref, u_scr = refs
    else:
        h_ref, wu_ref, wd_ref, x1_ref, g_ref, o_ref, u_scr = refs
    j = pl.program_id(1)

    def step(first):
        u = jnp.maximum(jnp.dot(h_ref[...], wu_ref[...], preferred_element_type=F32), 0.0)
        u_scr[...] = (u * u).astype(BF16)
        for n in range(0, D_MODEL, MLP_TN):
            part = jnp.dot(u_scr[...], wd_ref[:, n:n + MLP_TN], preferred_element_type=F32)
            if first:
                o_ref[:, n:n + MLP_TN] = part
            else:
                o_ref[:, n:n + MLP_TN] += part

    pl.when(j == 0)(functools.partial(step, True))
    pl.when(j > 0)(functools.partial(step, False))

    @pl.when(j == pl.num_programs(1) - 1)
    def _():
        x2 = x1_ref[...] + _rms(o_ref[...]) * g_ref[...]
        o_ref[...] = x2
        if has_next:
            hn_ref[...] = (_rms(x2) * gnext_ref[...]).astype(BF16)


def _mlp(h2, w_up, w_down, x1, g_post, g_next, tm):
    m = x1.shape[0]
    has_next = g_next is not None
    rowblk = pl.BlockSpec((tm, D_MODEL), lambda i, j: (i, 0))
    gain = pl.BlockSpec((1, D_MODEL), lambda i, j: (0, 0))
    in_specs = [rowblk,
                pl.BlockSpec((D_MODEL, MLP_TF), lambda i, j: (0, j)),
                pl.BlockSpec((MLP_TF, D_MODEL), lambda i, j: (j, 0)),
                rowblk, gain]
    args = [h2, w_up, w_down, x1, g_post]
    out_specs = [rowblk]
    out_shape = [jax.ShapeDtypeStruct((m, D_MODEL), F32)]
    if has_next:
        in_specs.append(gain)
        args.append(g_next)
        out_specs.append(rowblk)
        out_shape.append(jax.ShapeDtypeStruct((m, D_MODEL), BF16))
    outs = pl.pallas_call(
        functools.partial(_mlp_kernel, has_next=has_next),
        grid=(m // tm, D_FF // MLP_TF),
        in_specs=in_specs,
        out_specs=out_specs,
        out_shape=out_shape,
        scratch_shapes=[pltpu.VMEM((tm, MLP_TF), BF16)],
        compiler_params=pltpu.CompilerParams(
            dimension_semantics=("arbitrary", "arbitrary"), vmem_limit_bytes=VMEM_LIMIT),
        name="mlp",
    )(*args)
    return (outs[0], outs[1]) if has_next else (outs[0], None)


def _layer_weights(l, g_mix_pre, w_in, w_gla_gate_up, b_gla_gate, b_fox_f, g_gla_onorm,
                   g_mix_post, g_mlp_pre, g_mlp_post):
    row = lambda v: v.reshape(1, -1)
    w_t = _cast_w_in(jnp.swapaxes(w_in, 1, 2), l)
    w_small_t = jnp.concatenate([w_t[O_GLR:O_RG], w_t[O_FL:]], axis=0)
    return dict(
        g_pre=row(g_mix_pre[l]),
        w_t=w_t,
        w_small_t=jnp.pad(w_small_t, ((0, LANES - GLA_RANK - FOX_HEADS), (0, 0))),
        w_gu=jnp.pad(w_gla_gate_up[l], ((0, LANES - GLA_RANK), (0, 0))).astype(BF16),
        b_gate=row(b_gla_gate[l]),
        b_f=row(b_fox_f[l]),
        b_ft=b_fox_f[l].reshape(-1, 1),
        g_onorm=row(g_gla_onorm[l]),
        g_post=row(g_mix_post[l]),
        g_mlp_pre=row(g_mlp_pre[l]),
        g_mlp_post=row(g_mlp_post[l]),
    )


def _layer(x, h, wts, g_next, batch, seq, layer, kv_prev, tm_proj, tm, gla_rows, gla_chunk,
           s0=None, cache=None, cast=None):
    outs = _in_proj(h, wts, batch, seq, tm_proj, layer, kv_prev, cast or ())
    qg, kg, vg, rg, gate, logf, logft, qh, kh, vh, kf, vf = outs[:N_OUT]
    if cast:
        wts["w_up"], wts["w_down"], wts["w_out"] = outs[N_OUT:]
    hps = GLA_HEADS if gla_rows == gla_chunk else 2
    o_gla, s_new = _gla(qg, kg, vg, gate, rg, wts["g_onorm"], s0, batch, seq, gla_rows, gla_chunk, hps)
    if cache is None:
        lf = logft.reshape(FOX_HEADS, batch, seq).transpose(1, 0, 2)
        c_row = _cumsum(lf.reshape(batch * FOX_HEADS, seq))
        c_col = c_row.reshape(batch, FOX_HEADS, seq).transpose(0, 2, 1)
        o_fox = _fox_prompt(qh, kh, vh, c_row, c_col, batch, seq, tq=256, tk=512)
    else:
        k_hist, v_hist, logf_cache = cache
        past = logf_cache.shape[1]
        total = past + seq
        seg = past + FOX_TKC
        lf = jnp.concatenate([logf_cache, logf.reshape(batch, seq, FOX_HEADS)], axis=1)
        lf = jnp.pad(lf.transpose(0, 2, 1), ((0, 0), (0, 0), (0, seg - total)))
        c_row = _cumsum(lf.reshape(batch * FOX_HEADS, seg))
        c_q = c_row.reshape(batch, FOX_HEADS, seg)[:, :, past:total][..., None]
        o_fox = _fox_sample(qh, k_hist, v_hist, layer, kh, vh, c_row, c_q, batch, seq)
    x1, h2 = _out_proj(o_gla, o_fox, wts["w_out"], x, wts["g_post"], wts["g_mlp_pre"], tm)
    x2, h_next = _mlp(h2, wts["w_up"], wts["w_down"], x1, wts["g_mlp_post"], g_next, tm)
    return x2, h_next, (kf, vf), logf.reshape(batch, seq, FOX_HEADS), s_new


def kernel(x_prompt, x_sample, cache_fox_k, cache_fox_v, cache_fox_logf, state_gla, g_mix_pre, w_in,
           w_gla_gate_up, b_gla_gate, b_fox_f, g_gla_onorm, w_out, g_mix_post, g_mlp_pre, w_mlp_up,
           w_mlp_down, g_mlp_post):
    pb, pl_, _ = x_prompt.shape
    sb, sl, _ = x_sample.shape
    yp = x_prompt.reshape(pb * pl_, D_MODEL)
    ys = x_sample.reshape(sb * sl, D_MODEL)
    past = cache_fox_k.shape[2]
    k_hist = cache_fox_k.reshape(DEPTH, sb, past * FOX_HEADS, FOX_HD)
    v_hist = cache_fox_v.reshape(DEPTH, sb, past * FOX_HEADS, FOX_HD)
    g0 = g_mix_pre[0].reshape(1, -1)
    hp = _norm(yp, g0, 512)
    hs = _norm(ys, g0, sb * sl)
    kv_p = kv_s = None
    logf_p, logf_s, state_p, state_s = [], [], [], []
    for l in range(DEPTH):
        wts = _layer_weights(l, g_mix_pre, w_in, w_gla_gate_up, b_gla_gate, b_fox_f, g_gla_onorm,
                             g_mix_post, g_mlp_pre, g_mlp_post)
        g_next = g_mix_pre[l + 1].reshape(1, -1) if l + 1 < DEPTH else None
        yp, hp, kv_p, lf, st = _layer(yp, hp, wts, g_next, pb, pl_, l, kv_p, tm_proj=256, tm=512,
                                      gla_rows=1024, gla_chunk=CHUNK,
                                      cast=(w_mlp_up, w_mlp_down, w_out))
        logf_p.append(lf)
        state_p.append(st)
        ys, hs, kv_s, lf, st = _layer(ys, hs, wts, g_next, sb, sl, l, kv_s, tm_proj=sb * sl,
                                      tm=sb * sl, gla_rows=sl, gla_chunk=sl, s0=state_gla[l],
                                      cache=(k_hist, v_hist, cache_fox_logf[l]))
        logf_s.append(lf)
        state_s.append(st)
    kv5 = lambda a, b, s: a.reshape(DEPTH, b, s, FOX_HEADS, FOX_HD)
    return (yp.reshape(x_prompt.shape), ys.reshape(x_sample.shape),
            kv5(kv_p[0], pb, pl_), kv5(kv_p[1], pb, pl_), jnp.stack(logf_p), jnp.stack(state_p),
            kv5(kv_s[0], sb, sl), kv5(kv_s[1], sb, sl), jnp.stack(logf_s), jnp.stack(state_s))
```

```python
import functools

import jax
import jax.numpy as jnp
from jax import lax
from jax.experimental import pallas as pl
from jax.experimental.pallas import tpu as pltpu

F32 = jnp.float32
BF16 = jnp.bfloat16
HIGHEST = lax.Precision.HIGHEST

D_MODEL = 2048
DEPTH = 2
CHUNK = 64
GLA_HEADS = 4
GLA_DK = 128
GLA_DV = 256
GLA_QK = GLA_HEADS * GLA_DK
GLA_WIDTH = GLA_HEADS * GLA_DV
GLA_RANK = 16
GLA_TAU = 16.0
FOX_HEADS = 8
FOX_HD = 128
FOX_WIDTH = FOX_HEADS * FOX_HD
D_FF = 4 * D_MODEL
EPS = 1e-6
LOG2E = 1.4426950408889634
NEG = -1e30

O_GLR = 2 * GLA_QK + GLA_WIDTH
O_RG = O_GLR + GLA_RANK
O_FL = O_RG + GLA_WIDTH + 3 * FOX_WIDTH
PROJ_WIDTH = O_FL + FOX_HEADS
MAIN_WIDTH = PROJ_WIDTH - GLA_RANK - FOX_HEADS
LANES = 128
SMALL_FL = GLA_RANK

VMEM_LIMIT = 52 * 1024 * 1024
NT_DIMS = (((1,), (1,)), ((), ()))
TN_DIMS = (((0,), (0,)), ((), ()))


def _log_sigmoid(x):
    return jnp.minimum(x, 0.0) - jnp.log1p(jnp.exp(-jnp.abs(x)))


def _rms(x):
    return x * lax.rsqrt(jnp.mean(x * x, axis=-1, keepdims=True) + EPS)


def _lane_fold(x, op):
    parts = [x[:, i * LANES:(i + 1) * LANES] for i in range(x.shape[1] // LANES)]
    while len(parts) > 1:
        parts = [op(parts[i], parts[i + 1]) for i in range(0, len(parts), 2)]
    return parts[0]


def _norm_kernel(x_ref, g_ref, h_ref):
    h_ref[...] = (_rms(x_ref[...]) * g_ref[...]).astype(BF16)


def _norm(x, g, tm):
    m = x.shape[0]
    return pl.pallas_call(
        _norm_kernel,
        grid=(m // tm,),
        in_specs=[pl.BlockSpec((tm, D_MODEL), lambda i: (i, 0)),
                  pl.BlockSpec((1, D_MODEL), lambda i: (0, 0))],
        out_specs=pl.BlockSpec((tm, D_MODEL), lambda i: (i, 0)),
        out_shape=jax.ShapeDtypeStruct((m, D_MODEL), BF16),
        compiler_params=pltpu.CompilerParams(
            dimension_semantics=("arbitrary",), vmem_limit_bytes=VMEM_LIMIT),
        name="norm",
    )(x, g)


CAST_ROWS = 512


def _cast_kernel(w_ref, o_ref):
    o_ref[...] = w_ref[...].astype(BF16)


def _cast_w_in(w_t, layer):
    return pl.pallas_call(
        _cast_kernel,
        grid=(pl.cdiv(PROJ_WIDTH, CAST_ROWS),),
        in_specs=[pl.BlockSpec((None, CAST_ROWS, D_MODEL), lambda i: (layer, i, 0))],
        out_specs=pl.BlockSpec((CAST_ROWS, D_MODEL), lambda i: (i, 0)),
        out_shape=jax.ShapeDtypeStruct((PROJ_WIDTH, D_MODEL), BF16),
        compiler_params=pltpu.CompilerParams(
            dimension_semantics=("arbitrary",), vmem_limit_bytes=VMEM_LIMIT),
        name="cast_w_in",
    )(w_t)


IN_TN = 512
HEADS_PER_TN = IN_TN // FOX_HD
GLA_BLOCKS = O_GLR // IN_TN
N_IN = 7
N_OUT = 12
IN_PROJ_VMEM_LIMIT = 58 * 1024 * 1024


def _in_proj_kernel(*refs, nb, lt, n_alias, n_cast):
    h_ref, wt_ref, ws_ref, wgu_ref, bg_ref, bf_ref, bft_ref = refs[:N_IN]
    cast_in = refs[N_IN:N_IN + n_cast]
    outs = refs[N_IN + n_cast + n_alias:]
    (qg_ref, kg_ref, vg_ref, rg_ref, gate_ref, logf_ref, logft_ref,
     qh_ref, kh_ref, vh_ref, kf_ref, vf_ref) = outs[:N_OUT]
    tm = nb * lt
    h = h_ref[...]

    for src, dst in zip(cast_in, outs[N_OUT:]):
        dst[...] = src[...].astype(BF16)

    zs = lax.dot_general(h, ws_ref[...], NT_DIMS, preferred_element_type=F32)
    zst = lax.dot_general(ws_ref[...], h, NT_DIMS, preferred_element_type=F32)
    gpre = jnp.dot(zs.astype(BF16), wgu_ref[...], preferred_element_type=F32) + bg_ref[...]
    gate_ref[...] = _log_sigmoid(gpre) / GLA_TAU
    logf_ref[...] = _log_sigmoid(zs[:, SMALL_FL:SMALL_FL + FOX_HEADS] + bf_ref[...])
    logft_ref[...] = _log_sigmoid(zst[SMALL_FL:SMALL_FL + FOX_HEADS, :] + bft_ref[...])

    def mm(cb):
        r0 = cb * IN_TN + (GLA_RANK if cb >= GLA_BLOCKS else 0)
        return lax.dot_general(h, wt_ref[r0:r0 + IN_TN, :], NT_DIMS, preferred_element_type=F32)

    def heads(ref, val, half):
        for hh in range(HEADS_PER_TN):
            ref[:, half * HEADS_PER_TN + hh] = (
                val[:, hh * FOX_HD:(hh + 1) * FOX_HD].astype(BF16).reshape(nb, lt, FOX_HD))

    def rows(ref, val, half):
        for hh in range(HEADS_PER_TN):
            ref[pl.ds(half * HEADS_PER_TN + hh, tm, stride=FOX_HEADS), :] = (
                val[:, hh * FOX_HD:(hh + 1) * FOX_HD])

    qg_ref[...] = mm(0)
    kg_ref[...] = mm(1)
    for half in range(2):
        cols = slice(half * IN_TN, (half + 1) * IN_TN)
        vg_ref[:, cols] = mm(2 + half).astype(BF16)
        rg_ref[:, cols] = mm(4 + half)
        heads(qh_ref, mm(6 + half) * (FOX_HD ** -0.5 * LOG2E), half)
        val = mm(8 + half)
        heads(kh_ref, val, half)
        rows(kf_ref, val, half)
        val = mm(10 + half)
        heads(vh_ref, val, half)
        rows(vf_ref, val, half)


def _in_proj(h, wts, batch, seq, tm, layer, kv_prev, cast=()):
    m = h.shape[0]
    steps = m // tm
    lt = min(tm, seq)
    nb = tm // lt
    nt = seq // lt
    const = lambda i: (0, 0)
    rowblk = lambda w: pl.BlockSpec((tm, w), lambda i: (i, 0))
    head_shape = jax.ShapeDtypeStruct((batch, FOX_HEADS, seq, FOX_HD), BF16)
    head_spec = pl.BlockSpec((nb, FOX_HEADS, lt, FOX_HD), lambda i: (i // nt, 0, i % nt, 0))
    kv_shape = jax.ShapeDtypeStruct((DEPTH, m * FOX_HEADS, FOX_HD), F32)
    kv_spec = pl.BlockSpec((None, tm * FOX_HEADS, FOX_HD), lambda i: (layer, i, 0))
    out_shape = [
        jax.ShapeDtypeStruct((m, GLA_QK), F32), jax.ShapeDtypeStruct((m, GLA_QK), F32),
        jax.ShapeDtypeStruct((m, GLA_WIDTH), BF16), jax.ShapeDtypeStruct((m, GLA_WIDTH), F32),
        jax.ShapeDtypeStruct((m, GLA_QK), F32), jax.ShapeDtypeStruct((m, FOX_HEADS), F32),
        jax.ShapeDtypeStruct((FOX_HEADS, m), F32),
        head_shape, head_shape, head_shape, kv_shape, kv_shape,
    ]
    out_specs = [
        rowblk(GLA_QK), rowblk(GLA_QK), rowblk(GLA_WIDTH), rowblk(GLA_WIDTH),
        rowblk(GLA_QK), rowblk(FOX_HEADS),
        pl.BlockSpec((FOX_HEADS, tm), lambda i: (0, i)),
        head_spec, head_spec, head_spec, kv_spec, kv_spec,
    ]
    assert len(out_specs) == N_OUT
    in_specs = [
        rowblk(D_MODEL),
        pl.BlockSpec((PROJ_WIDTH, D_MODEL), const),
        pl.BlockSpec((LANES, D_MODEL), const),
        pl.BlockSpec((LANES, GLA_QK), const),
        pl.BlockSpec((1, GLA_QK), const),
        pl.BlockSpec((1, FOX_HEADS), const),
        pl.BlockSpec((FOX_HEADS, 1), const),
    ]
    args = [h, wts["w_t"], wts["w_small_t"], wts["w_gu"], wts["b_gate"], wts["b_f"], wts["b_ft"]]
    assert len(args) == N_IN
    for w in cast:
        _, r, c = w.shape
        in_specs.append(pl.BlockSpec((None, r // steps, c), lambda i: (layer, i, 0)))
        out_specs.append(pl.BlockSpec((r // steps, c), lambda i: (i, 0)))
        out_shape.append(jax.ShapeDtypeStruct((r, c), BF16))
        args.append(w)
    aliases = {}
    if kv_prev is not None:
        in_specs += [pl.BlockSpec(memory_space=pl.ANY)] * 2
        aliases = {len(args): N_OUT - 2, len(args) + 1: N_OUT - 1}
        args += list(kv_prev)
    return pl.pallas_call(
        functools.partial(_in_proj_kernel, nb=nb, lt=lt, n_alias=len(aliases), n_cast=len(cast)),
        grid=(steps,),
        in_specs=in_specs,
        out_specs=out_specs,
        out_shape=out_shape,
        input_output_aliases=aliases,
        compiler_params=pltpu.CompilerParams(
            dimension_semantics=("arbitrary",), vmem_limit_bytes=IN_PROJ_VMEM_LIMIT),
        name="in_proj",
    )(*args)


def _gla_kernel(*refs, chunk, nchunks, hps, has_s0):
    if has_s0:
        q_ref, k_ref, v_ref, g_ref, r_ref, gon_ref, s0_ref, o_ref, sout_ref, st = refs
    else:
        q_ref, k_ref, v_ref, g_ref, r_ref, gon_ref, o_ref, sout_ref, st = refs
    t = pl.program_id(2)

    @pl.when(t == 0)
    def _():
        for hh in range(hps):
            st[hh] = s0_ref[hh].T if has_s0 else jnp.zeros((GLA_DV, GLA_DK), F32)

    row = lax.broadcasted_iota(jnp.int32, (chunk, chunk), 0)
    col = lax.broadcasted_iota(jnp.int32, (chunk, chunk), 1)
    causal = col <= row
    tri = causal.astype(BF16)
    qscale = GLA_DK ** -0.5
    items = [(hh, c) for hh in range(hps) for c in range(nchunks)]
    sl = lambda c: pl.ds(c * chunk, chunk)
    dk = lambda hh: slice(hh * GLA_DK, (hh + 1) * GLA_DK)
    dv = lambda hh: slice(hh * GLA_DV, (hh + 1) * GLA_DV)

    def cumsum(g):
        b = None
        for _ in range(3):
            piece = g.astype(BF16)
            part = jnp.dot(tri, piece, preferred_element_type=F32)
            b = part if b is None else b + part
            g = g - piece.astype(F32)
        return b

    bs = [cumsum(g_ref[sl(c), dk(hh)]) for hh, c in items]
    b_lasts = [b[chunk - 1:chunk, :] for b in bs]
    q_is = [(q_ref[sl(c), dk(hh)] * qscale * jnp.exp(bs[n])).astype(BF16)
            for n, (hh, c) in enumerate(items)]
    k_is = [(k_ref[sl(c), dk(hh)] * jnp.exp(-bs[n])).astype(BF16) for n, (hh, c) in enumerate(items)]
    k_es = [(k_ref[sl(c), dk(hh)] * jnp.exp(b_lasts[n] - bs[n])).astype(BF16)
            for n, (hh, c) in enumerate(items)]
    atts = [lax.dot_general(q_is[n], k_is[n], NT_DIMS, preferred_element_type=F32)
            for n in range(len(items))]
    kv_ts = [lax.dot_general(v_ref[sl(c), dv(hh)], k_es[n], TN_DIMS, preferred_element_type=F32)
             for n, (hh, c) in enumerate(items)]
    atts = [jnp.where(causal, a, 0.0).astype(BF16) for a in atts]
    o_intra = [jnp.dot(atts[n], v_ref[sl(c), dv(hh)], preferred_element_type=F32)
               for n, (hh, c) in enumerate(items)]
    states = {}
    for n, (hh, c) in enumerate(items):
        prev = st[hh] if c == 0 else states[hh, c]
        states[hh, c] = prev
        states[hh, c + 1] = prev * jnp.exp(b_lasts[n]) + kv_ts[n]
    o_inter = [lax.dot_general(q_is[n], states[hh, c].astype(BF16), NT_DIMS,
                               preferred_element_type=F32) for n, (hh, c) in enumerate(items)]
    for n, (hh, c) in enumerate(items):
        r = r_ref[sl(c), dv(hh)]
        o = o_intra[n] + o_inter[n]
        o_ref[sl(c), dv(hh)] = (
            _rms(o) * gon_ref[:, dv(hh)] * (r * jax.nn.sigmoid(r))).astype(o_ref.dtype)
    for hh in range(hps):
        st[hh] = states[hh, nchunks]

    @pl.when(t == pl.num_programs(2) - 1)
    def _():
        for hh in range(hps):
            sout_ref[hh] = states[hh, nchunks].T


GLA_HEADS_PER_STEP = 4


def _gla(qg, kg, vg, gate, rg, g_onorm, s0, batch, seq, rows, chunk, hps):
    has_s0 = s0 is not None
    nchunks = rows // chunk
    r3 = lambda a: a.reshape(batch, seq, a.shape[-1])
    tok = lambda w: pl.BlockSpec((None, rows, hps * w), lambda b, h, t: (b, t, h))
    state_spec = pl.BlockSpec((None, hps, GLA_DK, GLA_DV), lambda b, h, t: (b, h, 0, 0))
    in_specs = [tok(GLA_DK), tok(GLA_DK), tok(GLA_DV), tok(GLA_DK), tok(GLA_DV),
                pl.BlockSpec((1, hps * GLA_DV), lambda b, h, t: (0, h))]
    args = [r3(qg), r3(kg), r3(vg), r3(gate), r3(rg), g_onorm]
    if has_s0:
        in_specs.append(state_spec)
        args.append(s0)
    return pl.pallas_call(
        functools.partial(_gla_kernel, chunk=chunk, nchunks=nchunks, hps=hps, has_s0=has_s0),
        grid=(batch, GLA_HEADS // hps, seq // rows),
        in_specs=in_specs,
        out_specs=[tok(GLA_DV), state_spec],
        out_shape=[jax.ShapeDtypeStruct((batch, seq, GLA_WIDTH), BF16),
                   jax.ShapeDtypeStruct((batch, GLA_HEADS, GLA_DK, GLA_DV), F32)],
        scratch_shapes=[pltpu.VMEM((hps, GLA_DV, GLA_DK), F32)],
        compiler_params=pltpu.CompilerParams(
            dimension_semantics=("arbitrary", "arbitrary", "arbitrary"), vmem_limit_bytes=VMEM_LIMIT),
        name="gla",
    )(*args)


def _cumsum_kernel(x_ref, o_ref, *, nblk):
    x = x_ref[...]
    length = nblk * LANES
    r = lax.broadcasted_iota(jnp.int32, (LANES, LANES), 0)
    c = lax.broadcasted_iota(jnp.int32, (LANES, LANES), 1)
    triu = (r <= c).astype(F32)
    li = lax.broadcasted_iota(jnp.int32, (length, LANES), 0)
    ji = lax.broadcasted_iota(jnp.int32, (length, LANES), 1)
    before = (li < ji * LANES).astype(F32)
    offs = jnp.dot(x, before, precision=HIGHEST, preferred_element_type=F32)
    for j in range(nblk):
        blk = x[:, j * LANES:(j + 1) * LANES]
        o_ref[:, j * LANES:(j + 1) * LANES] = (
            jnp.dot(blk, triu, precision=HIGHEST, preferred_element_type=F32) + offs[:, j:j + 1])


def _cumsum(xt):
    spec = pl.BlockSpec(xt.shape, lambda i: (0, 0))
    return pl.pallas_call(
        functools.partial(_cumsum_kernel, nblk=xt.shape[1] // LANES),
        grid=(1,),
        in_specs=[spec],
        out_specs=spec,
        out_shape=jax.ShapeDtypeStruct(xt.shape, F32),
        compiler_params=pltpu.CompilerParams(dimension_semantics=("arbitrary",)),
        name="cumsum",
    )(xt)


def _pick_row(block, h):
    sel = lax.broadcasted_iota(jnp.int32, (FOX_HEADS, 1), 0) == h
    return jnp.sum(jnp.where(sel, block, 0.0), axis=0, keepdims=True)


def _pick_col(block, h):
    sel = lax.broadcasted_iota(jnp.int32, (1, FOX_HEADS), 1) == h
    return jnp.sum(jnp.where(sel, block, 0.0), axis=1, keepdims=True)


def _fox_prompt_kernel(q_ref, k_ref, v_ref, crow_ref, ccol_ref, o_ref, s_scr, *, tq, tk, nq, hps):
    heads = range(hps)
    crow = [_pick_row(crow_ref[...], pl.program_id(1) * hps + hh) * LOG2E for hh in heads]
    ccol = [_pick_col(ccol_ref[...], pl.program_id(1) * hps + hh) * LOG2E for hh in heads]
    rows = lambda qi: slice(qi * tq, (qi + 1) * tq)

    def pieces(qi):
        end = (qi + 1) * tq
        return [(c0, min(c0 + tk, end)) for c0 in range(0, end, tk)]

    def pass1(hh, qi):
        q = q_ref[hh, rows(qi), :]
        cq = ccol[hh][rows(qi), :]
        mx = None
        ps = pieces(qi)
        for n, (c0, c1) in enumerate(ps):
            s = lax.dot_general(q, k_ref[hh, c0:c1, :], NT_DIMS, preferred_element_type=F32)
            s = s + (cq - crow[hh][:, c0:c1])
            if n == len(ps) - 1:
                qpos = lax.broadcasted_iota(jnp.int32, s.shape, 0) + qi * tq
                kpos = lax.broadcasted_iota(jnp.int32, s.shape, 1) + c0
                s = jnp.where(kpos <= qpos, s, NEG)
            s_scr[hh, qi, n, :, :c1 - c0] = s
            part = _lane_fold(s, jnp.maximum)
            mx = part if mx is None else jnp.maximum(mx, part)
        return jnp.max(mx, axis=1, keepdims=True)

    def pass2(hh, qi, m):
        lsum = acc = None
        for n, (c0, c1) in enumerate(pieces(qi)):
            p = jnp.exp2(s_scr[hh, qi, n, :, :c1 - c0] - m)
            pv = jnp.dot(p.astype(BF16), v_ref[hh, c0:c1, :], preferred_element_type=F32)
            part = _lane_fold(p, jnp.add)
            lsum = part if lsum is None else lsum + part
            acc = pv if acc is None else acc + pv
        o_ref[rows(qi), hh * FOX_HD:(hh + 1) * FOX_HD] = (
            acc / jnp.sum(lsum, axis=1, keepdims=True)).astype(o_ref.dtype)

    m_prev = [pass1(hh, 0) for hh in heads]
    for qi in range(1, nq):
        m_next = [pass1(hh, qi) for hh in heads]
        for hh in heads:
            pass2(hh, qi - 1, m_prev[hh])
        m_prev = m_next
    for hh in heads:
        pass2(hh, nq - 1, m_prev[hh])


FOX_HPS = 2


def _fox_prompt(qh, kh, vh, c_row, c_col, batch, seq, tq, tk):
    nq = seq // tq
    hps = FOX_HPS
    whole = pl.BlockSpec((None, hps, seq, FOX_HD), lambda b, h: (b, h, 0, 0))
    return pl.pallas_call(
        functools.partial(_fox_prompt_kernel, tq=tq, tk=tk, nq=nq, hps=hps),
        grid=(batch, FOX_HEADS // hps),
        in_specs=[
            whole, whole, whole,
            pl.BlockSpec((FOX_HEADS, seq), lambda b, h: (b, 0)),
            pl.BlockSpec((None, seq, FOX_HEADS), lambda b, h: (b, 0, 0)),
        ],
        out_specs=pl.BlockSpec((None, seq, hps * FOX_HD), lambda b, h: (b, 0, h)),
        out_shape=jax.ShapeDtypeStruct((batch, seq, FOX_WIDTH), BF16),
        scratch_shapes=[pltpu.VMEM((hps, nq, seq // tk, tq, tk), F32)],
        compiler_params=pltpu.CompilerParams(
            dimension_semantics=("arbitrary", "arbitrary"), vmem_limit_bytes=VMEM_LIMIT),
        name="fox_prompt",
    )(qh, kh, vh, c_row, c_col)


FOX_TKC = 512


def _fox_sample_kernel(q_ref, kc_ref, vc_ref, kn_ref, vn_ref, cc_ref, cn_ref, cq_ref, o_ref,
                       m_scr, l_scr, acc_scr, *, new):
    c = pl.program_id(1)

    @pl.when(c == 0)
    def _():
        m_scr[...] = jnp.full_like(m_scr, NEG)
        l_scr[...] = jnp.zeros_like(l_scr)
        acc_scr[...] = jnp.zeros_like(acc_scr)

    q = q_ref[...]
    cq = cq_ref[...] * LOG2E

    def logits(k, ck):
        s = jnp.einsum('hqd,hkd->hqk', q, k, preferred_element_type=F32)
        return s + (cq - ck * LOG2E)

    def update(s, v):
        m_old = m_scr[...]
        m_new = jnp.maximum(m_old, jnp.max(s, axis=2, keepdims=True))
        alpha = jnp.exp2(m_old - m_new)
        p = jnp.exp2(s - m_new)
        l_scr[...] = alpha * l_scr[...] + jnp.sum(p, axis=2, keepdims=True)
        pv = jnp.einsum('hqk,hkd->hqd', p.astype(BF16), v, preferred_element_type=F32)
        acc_scr[...] = alpha * acc_scr[...] + pv
        m_scr[...] = m_new

    def head_rows(ref):
        return jnp.stack([ref[pl.ds(h, FOX_TKC, stride=FOX_HEADS), :]
                          for h in range(FOX_HEADS)]).astype(BF16)

    update(logits(head_rows(kc_ref), cc_ref[...]), head_rows(vc_ref))

    @pl.when(c == pl.num_programs(1) - 1)
    def _():
        row = lax.broadcasted_iota(jnp.int32, (new, new), 0)
        col = lax.broadcasted_iota(jnp.int32, (new, new), 1)
        s = logits(kn_ref[...], cn_ref[:, :, :new])
        update(jnp.where((col <= row)[None], s, NEG), vn_ref[...])
        out = acc_scr[...] / l_scr[...]
        for h in range(FOX_HEADS):
            o_ref[:, h * FOX_HD:(h + 1) * FOX_HD] = out[h].astype(o_ref.dtype)


def _fox_sample(qh, k_hist, v_hist, layer, kh, vh, c_row, c_q, batch, new):
    past = k_hist.shape[2] // FOX_HEADS
    nkc = past // FOX_TKC
    heads_new = pl.BlockSpec((None, FOX_HEADS, new, FOX_HD), lambda b, c: (b, 0, 0, 0))
    hist = pl.BlockSpec((None, None, FOX_TKC * FOX_HEADS, FOX_HD), lambda b, c: (layer, b, c, 0))
    return pl.pallas_call(
        functools.partial(_fox_sample_kernel, new=new),
        grid=(batch, nkc),
        in_specs=[heads_new, hist, hist, heads_new, heads_new,
                  pl.BlockSpec((FOX_HEADS, 1, FOX_TKC), lambda b, c: (b, 0, c)),
                  pl.BlockSpec((FOX_HEADS, 1, FOX_TKC), lambda b, c: (b, 0, nkc)),
                  pl.BlockSpec((None, FOX_HEADS, new, 1), lambda b, c: (b, 0, 0, 0))],
        out_specs=pl.BlockSpec((None, new, FOX_WIDTH), lambda b, c: (b, 0, 0)),
        out_shape=jax.ShapeDtypeStruct((batch, new, FOX_WIDTH), BF16),
        scratch_shapes=[pltpu.VMEM((FOX_HEADS, new, 1), F32),
                        pltpu.VMEM((FOX_HEADS, new, 1), F32),
                        pltpu.VMEM((FOX_HEADS, new, FOX_HD), F32)],
        compiler_params=pltpu.CompilerParams(
            dimension_semantics=("arbitrary", "arbitrary"), vmem_limit_bytes=VMEM_LIMIT),
        name="fox_sample",
    )(qh, k_hist, v_hist, kh, vh, c_row[:, None, :], c_row[:, None, :], c_q)


OUT_PARTS = 4


def _out_proj_kernel(og_ref, of_ref, w_ref, x_ref, gpost_ref, gpre_ref, x1_ref, h2_ref):
    rows = og_ref.shape[0] // OUT_PARTS
    part = lambda a: slice(a * rows, (a + 1) * rows)
    ys = []
    for a in range(OUT_PARTS):
        y = jnp.dot(og_ref[part(a), :], w_ref[:GLA_WIDTH, :], preferred_element_type=F32)
        ys.append(y + jnp.dot(of_ref[part(a), :], w_ref[GLA_WIDTH:, :], preferred_element_type=F32))
    for a in range(OUT_PARTS):
        x1 = x_ref[part(a), :] + _rms(ys[a]) * gpost_ref[...]
        x1_ref[part(a), :] = x1
        h2_ref[part(a), :] = (_rms(x1) * gpre_ref[...]).astype(BF16)


def _out_proj(o_gla, o_fox, w_out, x, g_post, g_mlp_pre, tm):
    m = x.shape[0]
    const = lambda i: (0, 0)
    rowblk = lambda w: pl.BlockSpec((tm, w), lambda i: (i, 0))
    return pl.pallas_call(
        _out_proj_kernel,
        grid=(m // tm,),
        in_specs=[rowblk(GLA_WIDTH), rowblk(FOX_WIDTH),
                  pl.BlockSpec((D_MODEL, D_MODEL), const),
                  rowblk(D_MODEL),
                  pl.BlockSpec((1, D_MODEL), const), pl.BlockSpec((1, D_MODEL), const)],
        out_specs=[rowblk(D_MODEL), rowblk(D_MODEL)],
        out_shape=[jax.ShapeDtypeStruct((m, D_MODEL), F32), jax.ShapeDtypeStruct((m, D_MODEL), BF16)],
        compiler_params=pltpu.CompilerParams(
            dimension_semantics=("arbitrary",), vmem_limit_bytes=VMEM_LIMIT),
        name="out_proj",
    )(o_gla.reshape(m, GLA_WIDTH), o_fox.reshape(m, FOX_WIDTH), w_out, x, g_post, g_mlp_pre)


MLP_TF = 1024
MLP_TN = 512


def _mlp_kernel(*refs, has_next):
    if has_next:
        h_ref, wu_ref, wd_ref, x1_ref, g_ref, gnext_ref, o_ref, hn_ref, u_scr = refs
    else:
        h_ref, wu_ref, wd_ref, x1_ref, g_ref, o_ref, u_scr = refs
    j = pl.program_id(1)

    def step(first):
        u = jnp.maximum(jnp.dot(h_ref[...], wu_ref[...], preferred_element_type=F32), 0.0)
        u_scr[...] = (u * u).astype(BF16)
        for n in range(0, D_MODEL, MLP_TN):
            part = jnp.dot(u_scr[...], wd_ref[:, n:n + MLP_TN], preferred_element_type=F32)
            if first:
                o_ref[:, n:n + MLP_TN] = part
            else:
                o_ref[:, n:n + MLP_TN] += part

    pl.when(j == 0)(functools.partial(step, True))
    pl.when(j > 0)(functools.partial(step, False))

    @pl.when(j == pl.num_programs(1) - 1)
    def _():
        x2 = x1_ref[...] + _rms(o_ref[...]) * g_ref[...]
        o_ref[...] = x2
        if has_next:
            hn_ref[...] = (_rms(x2) * gnext_ref[...]).astype(BF16)


def _mlp(h2, w_up, w_down, x1, g_post, g_next, tm):
    m = x1.shape[0]
    has_next = g_next is not None
    rowblk = pl.BlockSpec((tm, D_MODEL), lambda i, j: (i, 0))
    gain = pl.BlockSpec((1, D_MODEL), lambda i, j: (0, 0))
    in_specs = [rowblk,
                pl.BlockSpec((D_MODEL, MLP_TF), lambda i, j: (0, j)),
                pl.BlockSpec((MLP_TF, D_MODEL), lambda i, j: (j, 0)),
                rowblk, gain]
    args = [h2, w_up, w_down, x1, g_post]
    out_specs = [rowblk]
    out_shape = [jax.ShapeDtypeStruct((m, D_MODEL), F32)]
    if has_next:
        in_specs.append(gain)
        args.append(g_next)
        out_specs.append(rowblk)
        out_shape.append(jax.ShapeDtypeStruct((m, D_MODEL), BF16))
    outs = pl.pallas_call(
        functools.partial(_mlp_kernel, has_next=has_next),
        grid=(m // tm, D_FF // MLP_TF),
        in_specs=in_specs,
        out_specs=out_specs,
        out_shape=out_shape,
        scratch_shapes=[pltpu.VMEM((tm, MLP_TF), BF16)],
        compiler_params=pltpu.CompilerParams(
            dimension_semantics=("arbitrary", "arbitrary"), vmem_limit_bytes=VMEM_LIMIT),
        name="mlp",
    )(*args)
    return (outs[0], outs[1]) if has_next else (outs[0], None)


def _layer_weights(l, g_mix_pre, w_in, w_gla_gate_up, b_gla_gate, b_fox_f, g_gla_onorm,
                   g_mix_post, g_mlp_pre, g_mlp_post):
    row = lambda v: v.reshape(1, -1)
    w_t = _cast_w_in(jnp.swapaxes(w_in, 1, 2), l)
    w_small_t = jnp.concatenate([w_t[O_GLR:O_RG], w_t[O_FL:]], axis=0)
    return dict(
        g_pre=row(g_mix_pre[l]),
        w_t=w_t,
        w_small_t=jnp.pad(w_small_t, ((0, LANES - GLA_RANK - FOX_HEADS), (0, 0))),
        w_gu=jnp.pad(w_gla_gate_up[l], ((0, LANES - GLA_RANK), (0, 0))).astype(BF16),
        b_gate=row(b_gla_gate[l]),
        b_f=row(b_fox_f[l]),
        b_ft=b_fox_f[l].reshape(-1, 1),
        g_onorm=row(g_gla_onorm[l]),
        g_post=row(g_mix_post[l]),
        g_mlp_pre=row(g_mlp_pre[l]),
        g_mlp_post=row(g_mlp_post[l]),
    )


def _layer(x, h, wts, g_next, batch, seq, layer, kv_prev, tm_proj, tm, gla_rows, gla_chunk,
           s0=None, cache=None, cast=None):
    outs = _in_proj(h, wts, batch, seq, tm_proj, layer, kv_prev, cast or ())
    qg, kg, vg, rg, gate, logf, logft, qh, kh, vh, kf, vf = outs[:N_OUT]
    if cast:
        wts["w_up"], wts["w_down"], wts["w_out"] = outs[N_OUT:]
    hps = GLA_HEADS if gla_rows == gla_chunk else GLA_HEADS_PER_STEP
    o_gla, s_new = _gla(qg, kg, vg, gate, rg, wts["g_onorm"], s0, batch, seq, gla_rows, gla_chunk, hps)
    if cache is None:
        lf = logft.reshape(FOX_HEADS, batch, seq).transpose(1, 0, 2)
        c_row = _cumsum(lf.reshape(batch * FOX_HEADS, seq))
        c_col = c_row.reshape(batch, FOX_HEADS, seq).transpose(0, 2, 1)
        o_fox = _fox_prompt(qh, kh, vh, c_row, c_col, batch, seq, tq=256, tk=512)
    else:
        k_hist, v_hist, logf_cache = cache
        past = logf_cache.shape[1]
        total = past + seq
        seg = past + FOX_TKC
        lf = jnp.concatenate([logf_cache, logf.reshape(batch, seq, FOX_HEADS)], axis=1)
        lf = jnp.pad(lf.transpose(0, 2, 1), ((0, 0), (0, 0), (0, seg - total)))
        c_row = _cumsum(lf.reshape(batch * FOX_HEADS, seg))
        c_q = c_row.reshape(batch, FOX_HEADS, seg)[:, :, past:total][..., None]
        o_fox = _fox_sample(qh, k_hist, v_hist, layer, kh, vh, c_row, c_q, batch, seq)
    x1, h2 = _out_proj(o_gla, o_fox, wts["w_out"], x, wts["g_post"], wts["g_mlp_pre"], tm)
    x2, h_next = _mlp(h2, wts["w_up"], wts["w_down"], x1, wts["g_mlp_post"], g_next, tm)
    return x2, h_next, (kf, vf), logf.reshape(batch, seq, FOX_HEADS), s_new


def kernel(x_prompt, x_sample, cache_fox_k, cache_fox_v, cache_fox_logf, state_gla, g_mix_pre, w_in,
           w_gla_gate_up, b_gla_gate, b_fox_f, g_gla_onorm, w_out, g_mix_post, g_mlp_pre, w_mlp_up,
           w_mlp_down, g_mlp_post):
    pb, pl_, _ = x_prompt.shape
    sb, sl, _ = x_sample.shape
    yp = x_prompt.reshape(pb * pl_, D_MODEL)
    ys = x_sample.reshape(sb * sl, D_MODEL)
    past = cache_fox_k.shape[2]
    k_hist = cache_fox_k.reshape(DEPTH, sb, past * FOX_HEADS, FOX_HD)
    v_hist = cache_fox_v.reshape(DEPTH, sb, past * FOX_HEADS, FOX_HD)
    g0 = g_mix_pre[0].reshape(1, -1)
    hp = _norm(yp, g0, 512)
    hs = _norm(ys, g0, sb * sl)
    kv_p = kv_s = None
    logf_p, logf_s, state_p, state_s = [], [], [], []
    for l in range(DEPTH):
        wts = _layer_weights(l, g_mix_pre, w_in, w_gla_gate_up, b_gla_gate, b_fox_f, g_gla_onorm,
                             g_mix_post, g_mlp_pre, g_mlp_post)
        g_next = g_mix_pre[l + 1].reshape(1, -1) if l + 1 < DEPTH else None
        yp, hp, kv_p, lf, st = _layer(yp, hp, wts, g_next, pb, pl_, l, kv_p, tm_proj=256, tm=512,
                                      gla_rows=1024, gla_chunk=CHUNK,
                                      cast=(w_mlp_up, w_mlp_down, w_out))
        logf_p.append(lf)
        state_p.append(st)
        ys, hs, kv_s, lf, st = _layer(ys, hs, wts, g_next, sb, sl, l, kv_s, tm_proj=sb * sl,
                                      tm=sb * sl, gla_rows=sl, gla_chunk=sl, s0=state_gla[l],
                                      cache=(k_hist, v_hist, cache_fox_logf[l]))
        logf_s.append(lf)
        state_s.append(st)
    kv5 = lambda a, b, s: a.reshape(DEPTH, b, s, FOX_HEADS, FOX_HD)
    return (yp.reshape(x_prompt.shape), ys.reshape(x_sample.shape),
            kv5(kv_p[0], pb, pl_), kv5(kv_p[1], pb, pl_), jnp.stack(logf_p), jnp.stack(state_p),
            kv5(kv_s[0], sb, sl), kv5(kv_s[1], sb, sl), jnp.stack(logf_s), jnp.stack(state_s))
```

```python
import functools

import jax
import jax.numpy as jnp
from jax import lax
from jax.experimental import pallas as pl
from jax.experimental.pallas import tpu as pltpu

F32 = jnp.float32
BF16 = jnp.bfloat16
HIGHEST = lax.Precision.HIGHEST

D_MODEL = 2048
DEPTH = 2
CHUNK = 64
GLA_HEADS = 4
GLA_DK = 128
GLA_DV = 256
GLA_QK = GLA_HEADS * GLA_DK
GLA_WIDTH = GLA_HEADS * GLA_DV
GLA_RANK = 16
GLA_TAU = 16.0
FOX_HEADS = 8
FOX_HD = 128
FOX_WIDTH = FOX_HEADS * FOX_HD
D_FF = 4 * D_MODEL
EPS = 1e-6
LOG2E = 1.4426950408889634
NEG = -1e30

O_GLR = 2 * GLA_QK + GLA_WIDTH
O_RG = O_GLR + GLA_RANK
O_FL = O_RG + GLA_WIDTH + 3 * FOX_WIDTH
PROJ_WIDTH = O_FL + FOX_HEADS
MAIN_WIDTH = PROJ_WIDTH - GLA_RANK - FOX_HEADS
LANES = 128
SMALL_FL = GLA_RANK

VMEM_LIMIT = 52 * 1024 * 1024
NT_DIMS = (((1,), (1,)), ((), ()))
TN_DIMS = (((0,), (0,)), ((), ()))


def _log_sigmoid(x):
    return jnp.minimum(x, 0.0) - jnp.log1p(jnp.exp(-jnp.abs(x)))


def _rms(x):
    return x * lax.rsqrt(jnp.mean(x * x, axis=-1, keepdims=True) + EPS)


def _lane_fold(x, op):
    parts = [x[:, i * LANES:(i + 1) * LANES] for i in range(x.shape[1] // LANES)]
    while len(parts) > 1:
        parts = [op(parts[i], parts[i + 1]) for i in range(0, len(parts), 2)]
    return parts[0]


def _norm_kernel(x_ref, g_ref, h_ref):
    h_ref[...] = (_rms(x_ref[...]) * g_ref[...]).astype(BF16)


def _norm(x, g, tm):
    m = x.shape[0]
    return pl.pallas_call(
        _norm_kernel,
        grid=(m // tm,),
        in_specs=[pl.BlockSpec((tm, D_MODEL), lambda i: (i, 0)),
                  pl.BlockSpec((1, D_MODEL), lambda i: (0, 0))],
        out_specs=pl.BlockSpec((tm, D_MODEL), lambda i: (i, 0)),
        out_shape=jax.ShapeDtypeStruct((m, D_MODEL), BF16),
        compiler_params=pltpu.CompilerParams(
            dimension_semantics=("arbitrary",), vmem_limit_bytes=VMEM_LIMIT),
        name="norm",
    )(x, g)


CAST_ROWS = 512


def _cast_kernel(w_ref, o_ref):
    o_ref[...] = w_ref[...].astype(BF16)


def _cast_w_in(w_t, layer):
    return pl.pallas_call(
        _cast_kernel,
        grid=(pl.cdiv(PROJ_WIDTH, CAST_ROWS),),
        in_specs=[pl.BlockSpec((None, CAST_ROWS, D_MODEL), lambda i: (layer, i, 0))],
        out_specs=pl.BlockSpec((CAST_ROWS, D_MODEL), lambda i: (i, 0)),
        out_shape=jax.ShapeDtypeStruct((PROJ_WIDTH, D_MODEL), BF16),
        compiler_params=pltpu.CompilerParams(
            dimension_semantics=("arbitrary",), vmem_limit_bytes=VMEM_LIMIT),
        name="cast_w_in",
    )(w_t)


IN_TN = 512
HEADS_PER_TN = IN_TN // FOX_HD
GLA_BLOCKS = O_GLR // IN_TN
N_IN = 7
N_OUT = 12
IN_PROJ_VMEM_LIMIT = 58 * 1024 * 1024


def _in_proj_kernel(*refs, nb, lt, n_alias, n_cast):
    h_ref, wt_ref, ws_ref, wgu_ref, bg_ref, bf_ref, bft_ref = refs[:N_IN]
    cast_in = refs[N_IN:N_IN + n_cast]
    outs = refs[N_IN + n_cast + n_alias:]
    (qg_ref, kg_ref, vg_ref, rg_ref, gate_ref, logf_ref, logft_ref,
     qh_ref, kh_ref, vh_ref, kf_ref, vf_ref) = outs[:N_OUT]
    tm = nb * lt
    h = h_ref[...]

    for src, dst in zip(cast_in, outs[N_OUT:]):
        dst[...] = src[...].astype(BF16)

    zs = lax.dot_general(h, ws_ref[...], NT_DIMS, preferred_element_type=F32)
    zst = lax.dot_general(ws_ref[...], h, NT_DIMS, preferred_element_type=F32)
    gpre = jnp.dot(zs.astype(BF16), wgu_ref[...], preferred_element_type=F32) + bg_ref[...]
    gate_ref[...] = _log_sigmoid(gpre) / GLA_TAU
    logf_ref[...] = _log_sigmoid(zs[:, SMALL_FL:SMALL_FL + FOX_HEADS] + bf_ref[...])
    logft_ref[...] = _log_sigmoid(zst[SMALL_FL:SMALL_FL + FOX_HEADS, :] + bft_ref[...])

    def mm(cb):
        r0 = cb * IN_TN + (GLA_RANK if cb >= GLA_BLOCKS else 0)
        return lax.dot_general(h, wt_ref[r0:r0 + IN_TN, :], NT_DIMS, preferred_element_type=F32)

    def heads(ref, val, half):
        for hh in range(HEADS_PER_TN):
            ref[:, half * HEADS_PER_TN + hh] = (
                val[:, hh * FOX_HD:(hh + 1) * FOX_HD].astype(BF16).reshape(nb, lt, FOX_HD))

    def rows(ref, val, half):
        for hh in range(HEADS_PER_TN):
            ref[pl.ds(half * HEADS_PER_TN + hh, tm, stride=FOX_HEADS), :] = (
                val[:, hh * FOX_HD:(hh + 1) * FOX_HD])

    qg_ref[...] = mm(0)
    kg_ref[...] = mm(1)
    for half in range(2):
        cols = slice(half * IN_TN, (half + 1) * IN_TN)
        vg_ref[:, cols] = mm(2 + half).astype(BF16)
        rg_ref[:, cols] = mm(4 + half)
        heads(qh_ref, mm(6 + half) * (FOX_HD ** -0.5 * LOG2E), half)
        val = mm(8 + half)
        heads(kh_ref, val, half)
        rows(kf_ref, val, half)
        val = mm(10 + half)
        heads(vh_ref, val, half)
        rows(vf_ref, val, half)


def _in_proj(h, wts, batch, seq, tm, layer, kv_prev, cast=()):
    m = h.shape[0]
    steps = m // tm
    lt = min(tm, seq)
    nb = tm // lt
    nt = seq // lt
    const = lambda i: (0, 0)
    rowblk = lambda w: pl.BlockSpec((tm, w), lambda i: (i, 0))
    head_shape = jax.ShapeDtypeStruct((batch, FOX_HEADS, seq, FOX_HD), BF16)
    head_spec = pl.BlockSpec((nb, FOX_HEADS, lt, FOX_HD), lambda i: (i // nt, 0, i % nt, 0))
    kv_shape = jax.ShapeDtypeStruct((DEPTH, m * FOX_HEADS, FOX_HD), F32)
    kv_spec = pl.BlockSpec((None, tm * FOX_HEADS, FOX_HD), lambda i: (layer, i, 0))
    out_shape = [
        jax.ShapeDtypeStruct((m, GLA_QK), F32), jax.ShapeDtypeStruct((m, GLA_QK), F32),
        jax.ShapeDtypeStruct((m, GLA_WIDTH), BF16), jax.ShapeDtypeStruct((m, GLA_WIDTH), F32),
        jax.ShapeDtypeStruct((m, GLA_QK), F32), jax.ShapeDtypeStruct((m, FOX_HEADS), F32),
        jax.ShapeDtypeStruct((FOX_HEADS, m), F32),
        head_shape, head_shape, head_shape, kv_shape, kv_shape,
    ]
    out_specs = [
        rowblk(GLA_QK), rowblk(GLA_QK), rowblk(GLA_WIDTH), rowblk(GLA_WIDTH),
        rowblk(GLA_QK), rowblk(FOX_HEADS),
        pl.BlockSpec((FOX_HEADS, tm), lambda i: (0, i)),
        head_spec, head_spec, head_spec, kv_spec, kv_spec,
    ]
    assert len(out_specs) == N_OUT
    in_specs = [
        rowblk(D_MODEL),
        pl.BlockSpec((PROJ_WIDTH, D_MODEL), const),
        pl.BlockSpec((LANES, D_MODEL), const),
        pl.BlockSpec((LANES, GLA_QK), const),
        pl.BlockSpec((1, GLA_QK), const),
        pl.BlockSpec((1, FOX_HEADS), const),
        pl.BlockSpec((FOX_HEADS, 1), const),
    ]
    args = [h, wts["w_t"], wts["w_small_t"], wts["w_gu"], wts["b_gate"], wts["b_f"], wts["b_ft"]]
    assert len(args) == N_IN
    for w in cast:
        _, r, c = w.shape
        in_specs.append(pl.BlockSpec((None, r // steps, c), lambda i: (layer, i, 0)))
        out_specs.append(pl.BlockSpec((r // steps, c), lambda i: (i, 0)))
        out_shape.append(jax.ShapeDtypeStruct((r, c), BF16))
        args.append(w)
    aliases = {}
    if kv_prev is not None:
        in_specs += [pl.BlockSpec(memory_space=pl.ANY)] * 2
        aliases = {len(args): N_OUT - 2, len(args) + 1: N_OUT - 1}
        args += list(kv_prev)
    return pl.pallas_call(
        functools.partial(_in_proj_kernel, nb=nb, lt=lt, n_alias=len(aliases), n_cast=len(cast)),
        grid=(steps,),
        in_specs=in_specs,
        out_specs=out_specs,
        out_shape=out_shape,
        input_output_aliases=aliases,
        compiler_params=pltpu.CompilerParams(
            dimension_semantics=("arbitrary",), vmem_limit_bytes=IN_PROJ_VMEM_LIMIT),
        name="in_proj",
    )(*args)


def _gla_kernel(*refs, chunk, nchunks, hps, has_s0):
    if has_s0:
        q_ref, k_ref, v_ref, g_ref, r_ref, gon_ref, s0_ref, o_ref, sout_ref, st = refs
    else:
        q_ref, k_ref, v_ref, g_ref, r_ref, gon_ref, o_ref, sout_ref, st = refs
    t = pl.program_id(2)

    @pl.when(t == 0)
    def _():
        for hh in range(hps):
            st[hh] = s0_ref[hh].T if has_s0 else jnp.zeros((GLA_DV, GLA_DK), F32)

    row = lax.broadcasted_iota(jnp.int32, (chunk, chunk), 0)
    col = lax.broadcasted_iota(jnp.int32, (chunk, chunk), 1)
    causal = col <= row
    tri = causal.astype(BF16)
    qscale = GLA_DK ** -0.5
    items = [(hh, c) for hh in range(hps) for c in range(nchunks)]
    sl = lambda c: pl.ds(c * chunk, chunk)
    dk = lambda hh: slice(hh * GLA_DK, (hh + 1) * GLA_DK)
    dv = lambda hh: slice(hh * GLA_DV, (hh + 1) * GLA_DV)

    def cumsum(g):
        b = None
        for _ in range(3):
            piece = g.astype(BF16)
            part = jnp.dot(tri, piece, preferred_element_type=F32)
            b = part if b is None else b + part
            g = g - piece.astype(F32)
        return b

    bs = [cumsum(g_ref[sl(c), dk(hh)]) for hh, c in items]
    b_lasts = [b[chunk - 1:chunk, :] for b in bs]
    q_is = [(q_ref[sl(c), dk(hh)] * qscale * jnp.exp(bs[n])).astype(BF16)
            for n, (hh, c) in enumerate(items)]
    k_is = [(k_ref[sl(c), dk(hh)] * jnp.exp(-bs[n])).astype(BF16) for n, (hh, c) in enumerate(items)]
    k_es = [(k_ref[sl(c), dk(hh)] * jnp.exp(b_lasts[n] - bs[n])).astype(BF16)
            for n, (hh, c) in enumerate(items)]
    atts = [lax.dot_general(q_is[n], k_is[n], NT_DIMS, preferred_element_type=F32)
            for n in range(len(items))]
    kv_ts = [lax.dot_general(v_ref[sl(c), dv(hh)], k_es[n], TN_DIMS, preferred_element_type=F32)
             for n, (hh, c) in enumerate(items)]
    atts = [jnp.where(causal, a, 0.0).astype(BF16) for a in atts]
    o_intra = [jnp.dot(atts[n], v_ref[sl(c), dv(hh)], preferred_element_type=F32)
               for n, (hh, c) in enumerate(items)]
    states = {}
    for n, (hh, c) in enumerate(items):
        prev = st[hh] if c == 0 else states[hh, c]
        states[hh, c] = prev
        states[hh, c + 1] = prev * jnp.exp(b_lasts[n]) + kv_ts[n]
    o_inter = [lax.dot_general(q_is[n], states[hh, c].astype(BF16), NT_DIMS,
                               preferred_element_type=F32) for n, (hh, c) in enumerate(items)]
    for n, (hh, c) in enumerate(items):
        r = r_ref[sl(c), dv(hh)]
        o = o_intra[n] + o_inter[n]
        o_ref[sl(c), dv(hh)] = (
            _rms(o) * gon_ref[:, dv(hh)] * (r * jax.nn.sigmoid(r))).astype(o_ref.dtype)
    for hh in range(hps):
        st[hh] = states[hh, nchunks]

    @pl.when(t == pl.num_programs(2) - 1)
    def _():
        for hh in range(hps):
            sout_ref[hh] = states[hh, nchunks].T


GLA_HEADS_PER_STEP = 4


def _gla(qg, kg, vg, gate, rg, g_onorm, s0, batch, seq, rows, chunk, hps):
    has_s0 = s0 is not None
    nchunks = rows // chunk
    r3 = lambda a: a.reshape(batch, seq, a.shape[-1])
    tok = lambda w: pl.BlockSpec((None, rows, hps * w), lambda b, h, t: (b, t, h))
    state_spec = pl.BlockSpec((None, hps, GLA_DK, GLA_DV), lambda b, h, t: (b, h, 0, 0))
    in_specs = [tok(GLA_DK), tok(GLA_DK), tok(GLA_DV), tok(GLA_DK), tok(GLA_DV),
                pl.BlockSpec((1, hps * GLA_DV), lambda b, h, t: (0, h))]
    args = [r3(qg), r3(kg), r3(vg), r3(gate), r3(rg), g_onorm]
    if has_s0:
        in_specs.append(state_spec)
        args.append(s0)
    return pl.pallas_call(
        functools.partial(_gla_kernel, chunk=chunk, nchunks=nchunks, hps=hps, has_s0=has_s0),
        grid=(batch, GLA_HEADS // hps, seq // rows),
        in_specs=in_specs,
        out_specs=[tok(GLA_DV), state_spec],
        out_shape=[jax.ShapeDtypeStruct((batch, seq, GLA_WIDTH), BF16),
                   jax.ShapeDtypeStruct((batch, GLA_HEADS, GLA_DK, GLA_DV), F32)],
        scratch_shapes=[pltpu.VMEM((hps, GLA_DV, GLA_DK), F32)],
        compiler_params=pltpu.CompilerParams(
            dimension_semantics=("arbitrary", "arbitrary", "arbitrary"), vmem_limit_bytes=VMEM_LIMIT),
        name="gla",
    )(*args)


def _cumsum_kernel(x_ref, o_ref, *, nblk):
    x = x_ref[...]
    length = nblk * LANES
    r = lax.broadcasted_iota(jnp.int32, (LANES, LANES), 0)
    c = lax.broadcasted_iota(jnp.int32, (LANES, LANES), 1)
    triu = (r <= c).astype(F32)
    li = lax.broadcasted_iota(jnp.int32, (length, LANES), 0)
    ji = lax.broadcasted_iota(jnp.int32, (length, LANES), 1)
    before = (li < ji * LANES).astype(F32)
    offs = jnp.dot(x, before, precision=HIGHEST, preferred_element_type=F32)
    for j in range(nblk):
        blk = x[:, j * LANES:(j + 1) * LANES]
        o_ref[:, j * LANES:(j + 1) * LANES] = (
            jnp.dot(blk, triu, precision=HIGHEST, preferred_element_type=F32) + offs[:, j:j + 1])


def _cumsum(xt):
    spec = pl.BlockSpec(xt.shape, lambda i: (0, 0))
    return pl.pallas_call(
        functools.partial(_cumsum_kernel, nblk=xt.shape[1] // LANES),
        grid=(1,),
        in_specs=[spec],
        out_specs=spec,
        out_shape=jax.ShapeDtypeStruct(xt.shape, F32),
        compiler_params=pltpu.CompilerParams(dimension_semantics=("arbitrary",)),
        name="cumsum",
    )(xt)


def _pick_row(block, h):
    sel = lax.broadcasted_iota(jnp.int32, (FOX_HEADS, 1), 0) == h
    return jnp.sum(jnp.where(sel, block, 0.0), axis=0, keepdims=True)


def _pick_col(block, h):
    sel = lax.broadcasted_iota(jnp.int32, (1, FOX_HEADS), 1) == h
    return jnp.sum(jnp.where(sel, block, 0.0), axis=1, keepdims=True)


def _fox_prompt_kernel(q_ref, k_ref, v_ref, crow_ref, ccol_ref, o_ref, s_scr, *, tq, tk, nq, hps):
    heads = range(hps)
    crow = [_pick_row(crow_ref[...], pl.program_id(1) * hps + hh) * LOG2E for hh in heads]
    ccol = [_pick_col(ccol_ref[...], pl.program_id(1) * hps + hh) * LOG2E for hh in heads]
    rows = lambda qi: slice(qi * tq, (qi + 1) * tq)

    def pieces(qi):
        end = (qi + 1) * tq
        return [(c0, min(c0 + tk, end)) for c0 in range(0, end, tk)]

    def pass1(hh, qi):
        q = q_ref[hh, rows(qi), :]
        cq = ccol[hh][rows(qi), :]
        mx = None
        ps = pieces(qi)
        for n, (c0, c1) in enumerate(ps):
            s = lax.dot_general(q, k_ref[hh, c0:c1, :], NT_DIMS, preferred_element_type=F32)
            s = s + (cq - crow[hh][:, c0:c1])
            if n == len(ps) - 1:
                qpos = lax.broadcasted_iota(jnp.int32, s.shape, 0) + qi * tq
                kpos = lax.broadcasted_iota(jnp.int32, s.shape, 1) + c0
                s = jnp.where(kpos <= qpos, s, NEG)
            s_scr[hh, qi, n, :, :c1 - c0] = s
            part = _lane_fold(s, jnp.maximum)
            mx = part if mx is None else jnp.maximum(mx, part)
        return jnp.max(mx, axis=1, keepdims=True)

    def pass2(hh, qi, m):
        lsum = acc = None
        for n, (c0, c1) in enumerate(pieces(qi)):
            p = jnp.exp2(s_scr[hh, qi, n, :, :c1 - c0] - m)
            pv = jnp.dot(p.astype(BF16), v_ref[hh, c0:c1, :], preferred_element_type=F32)
            part = _lane_fold(p, jnp.add)
            lsum = part if lsum is None else lsum + part
            acc = pv if acc is None else acc + pv
        o_ref[rows(qi), hh * FOX_HD:(hh + 1) * FOX_HD] = (
            acc / jnp.sum(lsum, axis=1, keepdims=True)).astype(o_ref.dtype)

    m_prev = [pass1(hh, 0) for hh in heads]
    for qi in range(1, nq):
        m_next = [pass1(hh, qi) for hh in heads]
        for hh in heads:
            pass2(hh, qi - 1, m_prev[hh])
        m_prev = m_next
    for hh in heads:
        pass2(hh, nq - 1, m_prev[hh])


FOX_HPS = 2


def _fox_prompt(qh, kh, vh, c_row, c_col, batch, seq, tq, tk):
    nq = seq // tq
    hps = FOX_HPS
    whole = pl.BlockSpec((None, hps, seq, FOX_HD), lambda b, h: (b, h, 0, 0))
    return pl.pallas_call(
        functools.partial(_fox_prompt_kernel, tq=tq, tk=tk, nq=nq, hps=hps),
        grid=(batch, FOX_HEADS // hps),
        in_specs=[
            whole, whole, whole,
            pl.BlockSpec((FOX_HEADS, seq), lambda b, h: (b, 0)),
            pl.BlockSpec((None, seq, FOX_HEADS), lambda b, h: (b, 0, 0)),
        ],
        out_specs=pl.BlockSpec((None, seq, hps * FOX_HD), lambda b, h: (b, 0, h)),
        out_shape=jax.ShapeDtypeStruct((batch, seq, FOX_WIDTH), BF16),
        scratch_shapes=[pltpu.VMEM((hps, nq, seq // tk, tq, tk), F32)],
        compiler_params=pltpu.CompilerParams(
            dimension_semantics=("arbitrary", "arbitrary"), vmem_limit_bytes=VMEM_LIMIT),
        name="fox_prompt",
    )(qh, kh, vh, c_row, c_col)


FOX_TKC = 512
FOX_SEQS_PER_STEP = 2


def _fox_sample_kernel(q_ref, kc_ref, vc_ref, kn_ref, vn_ref, cc_ref, cn_ref, cq_ref, o_ref,
                       m_scr, l_scr, acc_scr, *, new, nb):
    c = pl.program_id(1)
    seqs = range(nb)

    @pl.when(c == 0)
    def _():
        m_scr[...] = jnp.full_like(m_scr, NEG)
        l_scr[...] = jnp.zeros_like(l_scr)
        acc_scr[...] = jnp.zeros_like(acc_scr)

    q = [q_ref[b] for b in seqs]
    cq = [cq_ref[b] * LOG2E for b in seqs]
    ck_rows = lambda ref, b: ref[b * FOX_HEADS:(b + 1) * FOX_HEADS]

    def logits(b, k, ck):
        s = jnp.einsum('hqd,hkd->hqk', q[b], k, preferred_element_type=F32)
        return s + (cq[b] - ck * LOG2E)

    def update(b, s, v):
        m_old = m_scr[b]
        m_new = jnp.maximum(m_old, jnp.max(s, axis=2, keepdims=True))
        alpha = jnp.exp2(m_old - m_new)
        p = jnp.exp2(s - m_new)
        l_scr[b] = alpha * l_scr[b] + jnp.sum(p, axis=2, keepdims=True)
        pv = jnp.einsum('hqk,hkd->hqd', p.astype(BF16), v, preferred_element_type=F32)
        acc_scr[b] = alpha * acc_scr[b] + pv
        m_scr[b] = m_new

    def head_rows(ref, b):
        return jnp.stack([ref[b, pl.ds(h, FOX_TKC, stride=FOX_HEADS), :]
                          for h in range(FOX_HEADS)]).astype(BF16)

    ks = [head_rows(kc_ref, b) for b in seqs]
    vs = [head_rows(vc_ref, b) for b in seqs]
    ss = [logits(b, ks[b], ck_rows(cc_ref, b)) for b in seqs]
    for b in seqs:
        update(b, ss[b], vs[b])

    @pl.when(c == pl.num_programs(1) - 1)
    def _():
        row = lax.broadcasted_iota(jnp.int32, (new, new), 0)
        col = lax.broadcasted_iota(jnp.int32, (new, new), 1)
        for b in seqs:
            s = logits(b, kn_ref[b], ck_rows(cn_ref, b)[:, :, :new])
            update(b, jnp.where((col <= row)[None], s, NEG), vn_ref[b])
            out = acc_scr[b] / l_scr[b]
            for h in range(FOX_HEADS):
                o_ref[b, :, h * FOX_HD:(h + 1) * FOX_HD] = out[h].astype(o_ref.dtype)


def _fox_sample(qh, k_hist, v_hist, layer, kh, vh, c_row, c_q, batch, new):
    past = k_hist.shape[2] // FOX_HEADS
    nkc = past // FOX_TKC
    nb = FOX_SEQS_PER_STEP
    heads_new = pl.BlockSpec((nb, FOX_HEADS, new, FOX_HD), lambda b, c: (b, 0, 0, 0))
    hist = pl.BlockSpec((None, nb, FOX_TKC * FOX_HEADS, FOX_HD), lambda b, c: (layer, b, c, 0))
    return pl.pallas_call(
        functools.partial(_fox_sample_kernel, new=new, nb=nb),
        grid=(batch // nb, nkc),
        in_specs=[heads_new, hist, hist, heads_new, heads_new,
                  pl.BlockSpec((nb * FOX_HEADS, 1, FOX_TKC), lambda b, c: (b, 0, c)),
                  pl.BlockSpec((nb * FOX_HEADS, 1, FOX_TKC), lambda b, c: (b, 0, nkc)),
                  pl.BlockSpec((nb, FOX_HEADS, new, 1), lambda b, c: (b, 0, 0, 0))],
        out_specs=pl.BlockSpec((nb, new, FOX_WIDTH), lambda b, c: (b, 0, 0)),
        out_shape=jax.ShapeDtypeStruct((batch, new, FOX_WIDTH), BF16),
        scratch_shapes=[pltpu.VMEM((nb, FOX_HEADS, new, 1), F32),
                        pltpu.VMEM((nb, FOX_HEADS, new, 1), F32),
                        pltpu.VMEM((nb, FOX_HEADS, new, FOX_HD), F32)],
        compiler_params=pltpu.CompilerParams(
            dimension_semantics=("arbitrary", "arbitrary"), vmem_limit_bytes=VMEM_LIMIT),
        name="fox_sample",
    )(qh, k_hist, v_hist, kh, vh, c_row[:, None, :], c_row[:, None, :], c_q)


OUT_PARTS = 4


def _out_proj_kernel(og_ref, of_ref, w_ref, x_ref, gpost_ref, gpre_ref, x1_ref, h2_ref):
    rows = og_ref.shape[0] // OUT_PARTS
    part = lambda a: slice(a * rows, (a + 1) * rows)
    ys = []
    for a in range(OUT_PARTS):
        y = jnp.dot(og_ref[part(a), :], w_ref[:GLA_WIDTH, :], preferred_element_type=F32)
        ys.append(y + jnp.dot(of_ref[part(a), :], w_ref[GLA_WIDTH:, :], preferred_element_type=F32))
    for a in range(OUT_PARTS):
        x1 = x_ref[part(a), :] + _rms(ys[a]) * gpost_ref[...]
        x1_ref[part(a), :] = x1
        h2_ref[part(a), :] = (_rms(x1) * gpre_ref[...]).astype(BF16)


def _out_proj(o_gla, o_fox, w_out, x, g_post, g_mlp_pre, tm):
    m = x.shape[0]
    const = lambda i: (0, 0)
    rowblk = lambda w: pl.BlockSpec((tm, w), lambda i: (i, 0))
    return pl.pallas_call(
        _out_proj_kernel,
        grid=(m // tm,),
        in_specs=[rowblk(GLA_WIDTH), rowblk(FOX_WIDTH),
                  pl.BlockSpec((D_MODEL, D_MODEL), const),
                  rowblk(D_MODEL),
                  pl.BlockSpec((1, D_MODEL), const), pl.BlockSpec((1, D_MODEL), const)],
        out_specs=[rowblk(D_MODEL), rowblk(D_MODEL)],
        out_shape=[jax.ShapeDtypeStruct((m, D_MODEL), F32), jax.ShapeDtypeStruct((m, D_MODEL), BF16)],
        compiler_params=pltpu.CompilerParams(
            dimension_semantics=("arbitrary",), vmem_limit_bytes=VMEM_LIMIT),
        name="out_proj",
    )(o_gla.reshape(m, GLA_WIDTH), o_fox.reshape(m, FOX_WIDTH), w_out, x, g_post, g_mlp_pre)


MLP_TF = 1024
MLP_TN = 512


def _mlp_kernel(*refs, has_next):
    if has_next:
        h_ref, wu_ref, wd_ref, x1_ref, g_ref, gnext_ref, o_ref, hn_ref, u_scr = refs
    else:
        h_ref, wu_ref, wd_ref, x1_ref, g_ref, o_ref, u_scr = refs
    j = pl.program_id(1)

    def step(first):
        u = jnp.maximum(jnp.dot(h_ref[...], wu_ref[...], preferred_element_type=F32), 0.0)
        u_scr[...] = (u * u).astype(BF16)
        for n in range(0, D_MODEL, MLP_TN):
            part = jnp.dot(u_scr[...], wd_ref[:, n:n + MLP_TN], preferred_element_type=F32)
            if first:
                o_ref[:, n:n + MLP_TN] = part
            else:
                o_ref[:, n:n + MLP_TN] += part

    pl.when(j == 0)(functools.partial(step, True))
    pl.when(j > 0)(functools.partial(step, False))

    @pl.when(j == pl.num_programs(1) - 1)
    def _():
        x2 = x1_ref[...] + _rms(o_ref[...]) * g_ref[...]
        o_ref[...] = x2
        if has_next:
            hn_ref[...] = (_rms(x2) * gnext_ref[...]).astype(BF16)


def _mlp(h2, w_up, w_down, x1, g_post, g_next, tm):
    m = x1.shape[0]
    has_next = g_next is not None
    rowblk = pl.BlockSpec((tm, D_MODEL), lambda i, j: (i, 0))
    gain = pl.BlockSpec((1, D_MODEL), lambda i, j: (0, 0))
    in_specs = [rowblk,
                pl.BlockSpec((D_MODEL, MLP_TF), lambda i, j: (0, j)),
                pl.BlockSpec((MLP_TF, D_MODEL), lambda i, j: (j, 0)),
                rowblk, gain]
    args = [h2, w_up, w_down, x1, g_post]
    out_specs = [rowblk]
    out_shape = [jax.ShapeDtypeStruct((m, D_MODEL), F32)]
    if has_next:
        in_specs.append(gain)
        args.append(g_next)
        out_specs.append(rowblk)
        out_shape.append(jax.ShapeDtypeStruct((m, D_MODEL), BF16))
    outs = pl.pallas_call(
        functools.partial(_mlp_kernel, has_next=has_next),
        grid=(m // tm, D_FF // MLP_TF),
        in_specs=in_specs,
        out_specs=out_specs,
        out_shape=out_shape,
        scratch_shapes=[pltpu.VMEM((tm, MLP_TF), BF16)],
        compiler_params=pltpu.CompilerParams(
            dimension_semantics=("arbitrary", "arbitrary"), vmem_limit_bytes=VMEM_LIMIT),
        name="mlp",
    )(*args)
    return (outs[0], outs[1]) if has_next else (outs[0], None)


def _layer_weights(l, g_mix_pre, w_in, w_gla_gate_up, b_gla_gate, b_fox_f, g_gla_onorm,
                   g_mix_post, g_mlp_pre, g_mlp_post):
    row = lambda v: v.reshape(1, -1)
    w_t = _cast_w_in(jnp.swapaxes(w_in, 1, 2), l)
    w_small_t = jnp.concatenate([w_t[O_GLR:O_RG], w_t[O_FL:]], axis=0)
    return dict(
        g_pre=row(g_mix_pre[l]),
        w_t=w_t,
        w_small_t=jnp.pad(w_small_t, ((0, LANES - GLA_RANK - FOX_HEADS), (0, 0))),
        w_gu=jnp.pad(w_gla_gate_up[l], ((0, LANES - GLA_RANK), (0, 0))).astype(BF16),
        b_gate=row(b_gla_gate[l]),
        b_f=row(b_fox_f[l]),
        b_ft=b_fox_f[l].reshape(-1, 1),
        g_onorm=row(g_gla_onorm[l]),
        g_post=row(g_mix_post[l]),
        g_mlp_pre=row(g_mlp_pre[l]),
        g_mlp_post=row(g_mlp_post[l]),
    )


def _layer(x, h, wts, g_next, batch, seq, layer, kv_prev, tm_proj, tm, gla_rows, gla_chunk,
           s0=None, cache=None, cast=None):
    outs = _in_proj(h, wts, batch, seq, tm_proj, layer, kv_prev, cast or ())
    qg, kg, vg, rg, gate, logf, logft, qh, kh, vh, kf, vf = outs[:N_OUT]
    if cast:
        wts["w_up"], wts["w_down"], wts["w_out"] = outs[N_OUT:]
    hps = GLA_HEADS if gla_rows == gla_chunk else GLA_HEADS_PER_STEP
    o_gla, s_new = _gla(qg, kg, vg, gate, rg, wts["g_onorm"], s0, batch, seq, gla_rows, gla_chunk, hps)
    if cache is None:
        lf = logft.reshape(FOX_HEADS, batch, seq).transpose(1, 0, 2)
        c_row = _cumsum(lf.reshape(batch * FOX_HEADS, seq))
        c_col = c_row.reshape(batch, FOX_HEADS, seq).transpose(0, 2, 1)
        o_fox = _fox_prompt(qh, kh, vh, c_row, c_col, batch, seq, tq=256, tk=512)
    else:
        k_hist, v_hist, logf_cache = cache
        past = logf_cache.shape[1]
        total = past + seq
        seg = past + FOX_TKC
        lf = jnp.concatenate([logf_cache, logf.reshape(batch, seq, FOX_HEADS)], axis=1)
        lf = jnp.pad(lf.transpose(0, 2, 1), ((0, 0), (0, 0), (0, seg - total)))
        c_row = _cumsum(lf.reshape(batch * FOX_HEADS, seg))
        c_q = c_row.reshape(batch, FOX_HEADS, seg)[:, :, past:total][..., None]
        o_fox = _fox_sample(qh, k_hist, v_hist, layer, kh, vh, c_row, c_q, batch, seq)
    x1, h2 = _out_proj(o_gla, o_fox, wts["w_out"], x, wts["g_post"], wts["g_mlp_pre"], tm)
    x2, h_next = _mlp(h2, wts["w_up"], wts["w_down"], x1, wts["g_mlp_post"], g_next, tm)
    return x2, h_next, (kf, vf), logf.reshape(batch, seq, FOX_HEADS), s_new


def kernel(x_prompt, x_sample, cache_fox_k, cache_fox_v, cache_fox_logf, state_gla, g_mix_pre, w_in,
           w_gla_gate_up, b_gla_gate, b_fox_f, g_gla_onorm, w_out, g_mix_post, g_mlp_pre, w_mlp_up,
           w_mlp_down, g_mlp_post):
    pb, pl_, _ = x_prompt.shape
    sb, sl, _ = x_sample.shape
    yp = x_prompt.reshape(pb * pl_, D_MODEL)
    ys = x_sample.reshape(sb * sl, D_MODEL)
    past = cache_fox_k.shape[2]
    k_hist = cache_fox_k.reshape(DEPTH, sb, past * FOX_HEADS, FOX_HD)
    v_hist = cache_fox_v.reshape(DEPTH, sb, past * FOX_HEADS, FOX_HD)
    g0 = g_mix_pre[0].reshape(1, -1)
    hp = _norm(yp, g0, 512)
    hs = _norm(ys, g0, sb * sl)
    kv_p = kv_s = None
    logf_p, logf_s, state_p, state_s = [], [], [], []
    for l in range(DEPTH):
        wts = _layer_weights(l, g_mix_pre, w_in, w_gla_gate_up, b_gla_gate, b_fox_f, g_gla_onorm,
                             g_mix_post, g_mlp_pre, g_mlp_post)
        g_next = g_mix_pre[l + 1].reshape(1, -1) if l + 1 < DEPTH else None
        yp, hp, kv_p, lf, st = _layer(yp, hp, wts, g_next, pb, pl_, l, kv_p, tm_proj=256, tm=512,
                                      gla_rows=1024, gla_chunk=CHUNK,
                                      cast=(w_mlp_up, w_mlp_down, w_out))
        logf_p.append(lf)
        state_p.append(st)
        ys, hs, kv_s, lf, st = _layer(ys, hs, wts, g_next, sb, sl, l, kv_s, tm_proj=sb * sl,
                                      tm=sb * sl, gla_rows=sl, gla_chunk=sl, s0=state_gla[l],
                                      cache=(k_hist, v_hist, cache_fox_logf[l]))
        logf_s.append(lf)
        state_s.append(st)
    kv5 = lambda a, b, s: a.reshape(DEPTH, b, s, FOX_HEADS, FOX_HD)
    return (yp.reshape(x_prompt.shape), ys.reshape(x_sample.shape),
            kv5(kv_p[0], pb, pl_), kv5(kv_p[1], pb, pl_), jnp.stack(logf_p), jnp.stack(state_p),
            kv5(kv_s[0], sb, sl), kv5(kv_s[1], sb, sl), jnp.stack(logf_s), jnp.stack(state_s))
```
